```python
import math
import jax
import jax.numpy as jnp
from jax import lax
import numpy as np

D_MODEL = 2048
BATCH = 16
SEQ = 256
DEPTH = 4
DEC_BATCH = 8
DEC_SEQ = 2048
PAST_LEN = 512

GRID_W = 64
HD = 128
MIX_HEADS = D_MODEL // HD
A_Q_HEADS = MIX_HEADS // 2
A_KV_HEADS = A_Q_HEADS // 4
A_REP = A_Q_HEADS // A_KV_HEADS
B_HEADS = MIX_HEADS // 2
C_HEADS = MIX_HEADS // 2
D_HEADS = MIX_HEADS // 2
A_QW = A_Q_HEADS * HD
A_KVW = A_KV_HEADS * HD
B_W = B_HEADS * HD
C_W = C_HEADS * HD
D_W = D_HEADS * HD
MIX_W = A_QW + B_W
ATT_IN = A_QW + 2 * A_KVW + 3 * B_W
REC_IN = 4 * C_W + 4 * C_HEADS + 4 * D_W + 4 * D_HEADS
FF = 4 * D_MODEL
WINDOW = 128
WBLK = 128
QBLK = 128
NA_ROWS = 8
NA_COLS = 16
NA_QCOLS = 16
NA_REG = NA_QCOLS + NA_COLS
CHUNK = 64
CONV_W = 3
ROPE_BASE = 10000.0
EPS = 1e-6
N_ATT = (DEPTH + 1) // 2
N_REC = DEPTH // 2
ATTN_SCALE = HD ** -0.5

kernel_name = 'hybrid_diffusion_prefix_trunk_step'


def rmsnorm(x, g):
    xf = x.astype(jnp.float32)
    y = xf * lax.rsqrt(jnp.mean(xf * xf, axis=-1, keepdims=True) + EPS)
    return (y * g.astype(jnp.float32)).astype(x.dtype)


def l2norm(x):
    return x * lax.rsqrt(jnp.sum(x * x, axis=-1, keepdims=True) + EPS)


def split_cols(p, sizes):
    return jnp.split(p, [int(s) for s in np.cumsum(sizes)[:-1]], axis=-1)


def adaln(cvec, w, b):
    mod = jax.nn.silu(cvec) @ w + b
    return [m[:, None, :] for m in jnp.split(mod, 6, axis=-1)]


def joint_softmax(segments, sink=None):
    sizes = [s.shape[-1] for s in segments]
    parts = [s.astype(jnp.float32) for s in segments]
    if sink is not None:
        parts.append(jnp.broadcast_to(sink.astype(jnp.float32), parts[0].shape[:-1] + (1,)))
    p = jax.nn.softmax(jnp.concatenate(parts, axis=-1), axis=-1)
    offs = [int(o) for o in np.cumsum([0] + sizes)]
    return [p[..., offs[i]:offs[i + 1]] for i in range(len(sizes))]


def axial_rope(x):
    S = x.shape[1]
    t = jnp.arange(S)
    row = (t // GRID_W).astype(jnp.float32)
    col = (t % GRID_W).astype(jnp.float32)
    nf = HD // 4
    inv = jnp.exp(-math.log(ROPE_BASE) * jnp.arange(nf, dtype=jnp.float32) / nf)
    shape = (1, S) + (1,) * (x.ndim - 3) + (nf,)

    def rot(xh, pos):
        ang = pos[:, None] * inv
        cos = jnp.cos(ang).reshape(shape)
        sin = jnp.sin(ang).reshape(shape)
        x1, x2 = xh[..., :nf], xh[..., nf:]
        return jnp.concatenate([x1 * cos - x2 * sin, x1 * sin + x2 * cos], axis=-1)

    xf = x.astype(jnp.float32)
    return jnp.concatenate([rot(xf[..., :HD // 2], row), rot(xf[..., HD // 2:], col)], axis=-1).astype(x.dtype)


def ctx_attention(q, k, v, sink):
    B, T, G, R, _ = q.shape
    nb = T // QBLK
    qb = jnp.moveaxis(q.reshape(B, nb, QBLK, G, R, HD), 1, 0)

    def one(qblk):
        s = jnp.einsum('bqgrd,bkgd->bgrqk', qblk, k) * ATTN_SCALE
        (p,) = joint_softmax([s], sink)
        return jnp.einsum('bgrqk,bkgd->bqgrd', p.astype(v.dtype), v)

    o = lax.map(one, qb)
    return jnp.moveaxis(o, 0, 1).reshape(B, T, G, R, HD)


def window_attention(q, k, v, k_ctx, v_ctx, sink):
    B, S, G, R, _ = q.shape
    nb = S // WBLK
    qb = q.reshape(B, nb, WBLK, G, R, HD)

    def bands(t):
        tp = jnp.pad(t, ((0, 0), (WBLK, WBLK), (0, 0), (0, 0))).reshape(B, nb + 2, WBLK, G, HD)
        return jnp.concatenate([tp[:, :nb], tp[:, 1:nb + 1], tp[:, 2:]], axis=2)

    kw, vw = bands(k), bands(v)
    blk = jnp.arange(nb)[:, None, None]
    qpos = blk * WBLK + jnp.arange(WBLK)[None, :, None]
    kpos = (blk - 1) * WBLK + jnp.arange(3 * WBLK)[None, None, :]
    valid = (jnp.abs(qpos - kpos) <= WINDOW) & (kpos >= 0) & (kpos < S)
    s_lat = jnp.einsum('bnqgrd,bnkgd->bngrqk', qb, kw) * ATTN_SCALE
    s_lat = jnp.where(valid[None, :, None, None], s_lat, -jnp.inf)
    s_ctx = jnp.einsum('bnqgrd,bpgd->bngrqp', qb, k_ctx) * ATTN_SCALE
    p_lat, p_ctx = joint_softmax([s_lat, s_ctx], sink)
    o = (jnp.einsum('bngrqk,bnkgd->bnqgrd', p_lat.astype(v.dtype), vw)
         + jnp.einsum('bngrqp,bpgd->bnqgrd', p_ctx.astype(v.dtype), v_ctx))
    return o.reshape(B, S, G, R, HD)


def neighbourhood_attention(q, k, v, k_ctx, v_ctx, rel_bias):
    B, S, H, _ = q.shape
    rows = S // GRID_W
    kr = min(NA_ROWS, rows)
    ncb = GRID_W // NA_QCOLS
    reg_start = np.clip(np.arange(ncb) * NA_QCOLS - NA_COLS // 2, 0, GRID_W - NA_REG)
    col_idx = reg_start[:, None] + np.arange(NA_REG)[None, :]
    qcol = np.arange(ncb)[:, None] * NA_QCOLS + np.arange(NA_QCOLS)[None, :]
    wstart = np.clip(qcol - NA_COLS // 2, 0, GRID_W - NA_COLS)
    kcol = col_idx[:, None, :]
    col_mask = jnp.asarray((kcol >= wstart[..., None]) & (kcol < wstart[..., None] + NA_COLS))
    dc_idx = jnp.asarray(np.clip(kcol - qcol[..., None] + NA_COLS - 1, 0, 2 * NA_COLS - 2))
    col_idx = jnp.asarray(col_idx)
    qg = q.reshape(B, rows, ncb, NA_QCOLS, H, HD)
    kg = k.reshape(B, rows, GRID_W, H, HD)
    vg = v.reshape(B, rows, GRID_W, H, HD)

    def one_row(r):
        rs = jnp.clip(r - kr // 2, 0, rows - kr)
        k_reg = lax.dynamic_slice_in_dim(kg, rs, kr, axis=1)[:, :, col_idx]
        v_reg = lax.dynamic_slice_in_dim(vg, rs, kr, axis=1)[:, :, col_idx]
        q_row = lax.dynamic_index_in_dim(qg, r, axis=1, keepdims=False)
        s_lat = jnp.einsum('bjqhd,bijkhd->bhjqik', q_row, k_reg) * ATTN_SCALE
        dr = rs + jnp.arange(kr) - r + (NA_ROWS - 1)
        bias = jnp.transpose(rel_bias[:, dr][:, :, dc_idx], (0, 2, 3, 1, 4))
        s_lat = jnp.where(col_mask[:, :, None, :], s_lat + bias, -jnp.inf)
        s_lat = s_lat.reshape(B, H, ncb, NA_QCOLS, kr * NA_REG)
        s_ctx = jnp.einsum('bjqhd,bphd->bhjqp', q_row, k_ctx) * ATTN_SCALE
        p_lat, p_ctx = joint_softmax([s_lat, s_ctx])
        p_lat = p_lat.reshape(B, H, ncb, NA_QCOLS, kr, NA_REG).astype(v.dtype)
        return (jnp.einsum('bhjqik,bijkhd->bjqhd', p_lat, v_reg)
                + jnp.einsum('bhjqp,bphd->bjqhd', p_ctx.astype(v.dtype), v_ctx))

    o = lax.map(one_row, jnp.arange(rows))
    return jnp.moveaxis(o, 0, 1).reshape(B, S, H, HD)


def centred_conv(x, w):
    return lax.conv_general_dilated(x, w[:, None, :], window_strides=(1,),
                                    padding=[(CONV_W // 2, CONV_W // 2)],
                                    dimension_numbers=('NWC', 'WIO', 'NWC'),
                                    feature_group_count=x.shape[-1])


def to_chunks(x):
    B, T = x.shape[:2]
    x = x.reshape((B, T // CHUNK, CHUNK) + x.shape[2:])
    return jnp.moveaxis(jnp.moveaxis(x, 1, 0), 2, 3)


def from_chunks(o):
    nc, B = o.shape[:2]
    o = jnp.moveaxis(jnp.moveaxis(o, 3, 2), 0, 1)
    return o.reshape((B, nc * CHUNK) + o.shape[3:])


def gated_delta_chunked(q, k, v, beta, g, S0):
    incl = jnp.tril(jnp.ones((CHUNK, CHUNK), bool))
    strict = jnp.tril(jnp.ones((CHUNK, CHUNK), bool), -1)
    eye = jnp.eye(CHUNK, dtype=jnp.float32)

    def body(S, inp):
        qc, kc, vc, bc, gc = inp
        gcum = jnp.cumsum(gc, axis=-1)
        decay = jnp.exp(jnp.where(incl, gcum[..., :, None] - gcum[..., None, :], -jnp.inf))
        kb = kc * bc[..., None]
        a = jnp.where(strict, jnp.einsum('bhid,bhjd->bhij', kb, kc) * decay, 0.0) + eye
        rhs = jnp.concatenate([vc * bc[..., None], kb * jnp.exp(gcum)[..., None]], axis=-1)
        uw = lax.linalg.triangular_solve(a, rhs, left_side=True, lower=True, unit_diagonal=True)
        u, w = uw[..., :HD], uw[..., HD:]
        v_new = u - jnp.einsum('bhlk,bhkv->bhlv', w, S)
        o = (jnp.einsum('bhlk,bhkv->bhlv', qc * jnp.exp(gcum)[..., None], S)
             + jnp.einsum('bhij,bhjv->bhiv', jnp.einsum('bhid,bhjd->bhij', qc, kc) * decay, v_new))
        glast = gcum[..., -1]
        S = (S * jnp.exp(glast)[..., None, None]
             + jnp.einsum('bhlk,bhlv->bhkv', kc * jnp.exp(glast[..., None] - gcum)[..., None], v_new))
        return S, o

    S, o = lax.scan(body, S0, (to_chunks(q), to_chunks(k), to_chunks(v), to_chunks(beta), to_chunks(g)))
    return from_chunks(o), S


def mlstm_chunked(q, k, v, ig, lf, C0, n0, m0):
    incl = jnp.tril(jnp.ones((CHUNK, CHUNK), bool))

    def body(carry, inp):
        C, n, m = carry
        qc, kc, vc, ic, fc = inp
        b = jnp.cumsum(fc, axis=-1)
        dmat = jnp.where(incl, b[..., :, None] - b[..., None, :] + ic[..., None, :], -jnp.inf)
        inter = b + m[..., None]
        mt = jnp.maximum(inter, jnp.max(dmat, axis=-1))
        w_intra = jnp.exp(dmat - mt[..., None])
        w_inter = jnp.exp(inter - mt)
        qk = jnp.einsum('bhid,bhjd->bhij', qc, kc) * w_intra
        num = (w_inter[..., None] * jnp.einsum('bhlk,bhkv->bhlv', qc, C)
               + jnp.einsum('bhij,bhjv->bhiv', qk, vc))
        den = w_inter * jnp.einsum('bhlk,bhk->bhl', qc, n) + jnp.sum(qk, axis=-1)
        hc = num / jnp.maximum(jnp.abs(den), jnp.exp(-mt))[..., None]
        m_new = mt[..., -1]
        wk = jnp.exp(b[..., -1:] - b + ic - m_new[..., None])
        dstate = jnp.exp(b[..., -1] + m - m_new)
        C = dstate[..., None, None] * C + jnp.einsum('bhlk,bhlv->bhkv', kc * wk[..., None], vc)
        n = dstate[..., None] * n + jnp.einsum('bhl,bhlk->bhk', wk, kc)
        return (C, n, m_new), hc

    (C, n, m), h = lax.scan(body, (C0, n0, m0),
                            (to_chunks(q), to_chunks(k), to_chunks(v), to_chunks(ig), to_chunks(lf)))
    return from_chunks(h), (C, n, m)


def split_att(h, w_in):
    B, T, _ = h.shape
    aq, ak, av, bq, bk, bv = split_cols(h @ w_in, [A_QW, A_KVW, A_KVW, B_W, B_W, B_W])
    return (aq.reshape(B, T, A_KV_HEADS, A_REP, HD), ak.reshape(B, T, A_KV_HEADS, HD),
            av.reshape(B, T, A_KV_HEADS, HD), bq.reshape(B, T, B_HEADS, HD),
            bk.reshape(B, T, B_HEADS, HD), bv.reshape(B, T, B_HEADS, HD))


def att_mixer_ctx(h, w_in, w_out, sink):
    B, T, _ = h.shape
    aq, ak, av, bq, bk, bv = split_att(h, w_in)
    oa = ctx_attention(aq, ak, av, sink[:, :, None, None])
    ob = ctx_attention(bq[:, :, :, None], bk, bv, None)
    y = jnp.concatenate([oa.reshape(B, T, A_QW), ob.reshape(B, T, B_W)], axis=-1) @ w_out
    return y, (ak, av, bk, bv)


def att_mixer_lat(h, w_in, w_out, sink, rel_bias, ka_c, va_c, kb_c, vb_c):
    B, S, _ = h.shape
    aq, ak, av, bq, bk, bv = split_att(h, w_in)
    oa = window_attention(axial_rope(aq), axial_rope(ak), av, ka_c, va_c, sink[:, :, None, None])
    ob = neighbourhood_attention(bq, bk, bv, kb_c, vb_c, rel_bias)
    return jnp.concatenate([oa.reshape(B, S, A_QW), ob.reshape(B, S, B_W)], axis=-1) @ w_out


def rec_mixer(h, w_in, w_out, conv_w, a_log, dt_bias, c_norm, b_i, b_f, d_norm, s_c0, s_dc0, s_dn0, s_dm0):
    B, T, _ = h.shape
    f32 = jnp.float32
    p = (h @ w_in).astype(f32)
    c_qkv, c_z, c_beta, c_a, d_q, d_k, d_v, d_o, d_i, d_f = split_cols(
        p, [3 * C_W, C_W, 2 * C_HEADS, 2 * C_HEADS, D_W, D_W, D_W, D_W, 2 * D_HEADS, 2 * D_HEADS])
    rev = lambda t: t[:, ::-1]
    c_qkv = jax.nn.silu(centred_conv(c_qkv, conv_w.astype(f32)))
    cq, ck, cv = [t.reshape(B, T, C_HEADS, HD) for t in jnp.split(c_qkv, 3, axis=-1)]
    cq = l2norm(cq) * ATTN_SCALE
    ck = l2norm(ck)
    beta = jax.nn.sigmoid(c_beta).reshape(B, T, 2, C_HEADS)
    g = -jnp.exp(a_log.astype(f32)) * jax.nn.softplus(c_a.reshape(B, T, 2, C_HEADS) + dt_bias.astype(f32))
    o_f, sc_f = gated_delta_chunked(cq, ck, cv, beta[:, :, 0], g[:, :, 0], s_c0[:, 0].astype(f32))
    o_b, sc_b = gated_delta_chunked(rev(cq), rev(ck), rev(cv), rev(beta[:, :, 1]), rev(g[:, :, 1]),
                                    s_c0[:, 1].astype(f32))
    oc = rmsnorm(o_f + rev(o_b), c_norm) * jax.nn.silu(c_z.reshape(B, T, C_HEADS, HD))
    dq = d_q.reshape(B, T, D_HEADS, HD) * ATTN_SCALE
    dk = d_k.reshape(B, T, D_HEADS, HD)
    dv = d_v.reshape(B, T, D_HEADS, HD)
    ig = d_i.reshape(B, T, 2, D_HEADS) + b_i.astype(f32)
    lf = jax.nn.log_sigmoid(d_f.reshape(B, T, 2, D_HEADS) + b_f.astype(f32))
    h_f, (C_f, n_f, m_f) = mlstm_chunked(dq, dk, dv, ig[:, :, 0], lf[:, :, 0], s_dc0[:, 0].astype(f32),
                                         s_dn0[:, 0].astype(f32), s_dm0[:, 0].astype(f32))
    h_b, (C_b, n_b, m_b) = mlstm_chunked(rev(dq), rev(dk), rev(dv), rev(ig[:, :, 1]), rev(lf[:, :, 1]),
                                         s_dc0[:, 1].astype(f32), s_dn0[:, 1].astype(f32),
                                         s_dm0[:, 1].astype(f32))
    od = rmsnorm(h_f + rev(h_b), d_norm) * jax.nn.sigmoid(d_o.reshape(B, T, D_HEADS, HD))
    y = jnp.concatenate([oc.reshape(B, T, C_W), od.reshape(B, T, D_W)], axis=-1).astype(h.dtype) @ w_out
    states = (jnp.stack([sc_f, sc_b], axis=1), jnp.stack([C_f, C_b], axis=1),
              jnp.stack([n_f, n_b], axis=1), jnp.stack([m_f, m_b], axis=1))
    return y, states


def mlp(h, w_up, w_down):
    return jnp.square(jax.nn.relu(h @ w_up)) @ w_down


def setup_inputs(seed: int = 0) -> dict:
    key = jax.random.key(seed)
    keys = jax.random.split(key, 48)
    counter = [0]

    def nxt():
        counter[0] += 1
        return keys[counter[0] - 1]

    def nrm(shape, s=1.0):
        return jax.random.normal(nxt(), shape, jnp.float32) * s

    dt = jnp.exp(jax.random.uniform(nxt(), (N_REC, 2, C_HEADS), jnp.float32, math.log(1e-3), math.log(1e-1)))
    return {
        'x_prompt': nrm((BATCH, SEQ, D_MODEL)),
        'x_sample': nrm((DEC_BATCH, DEC_SEQ, D_MODEL)),
        'cache_a_k': nrm((DEC_BATCH, N_ATT, PAST_LEN, A_KV_HEADS, HD)),
        'cache_a_v': nrm((DEC_BATCH, N_ATT, PAST_LEN, A_KV_HEADS, HD)),
        'cache_b_k': nrm((DEC_BATCH, N_ATT, PAST_LEN, B_HEADS, HD)),
        'cache_b_v': nrm((DEC_BATCH, N_ATT, PAST_LEN, B_HEADS, HD)),
        'state_c': nrm((DEC_BATCH, N_REC, 2, C_HEADS, HD, HD), 0.1),
        'state_d_c': nrm((DEC_BATCH, N_REC, 2, D_HEADS, HD, HD), 0.1),
        'state_d_n': nrm((DEC_BATCH, N_REC, 2, D_HEADS, HD), 0.1),
        'state_d_m': nrm((DEC_BATCH, N_REC, 2, D_HEADS)),
        'c': nrm((DEC_BATCH, D_MODEL)),
        'c_ctx': nrm((D_MODEL,)),
        'w_mod': nrm((DEPTH, D_MODEL, 6 * D_MODEL), 0.5 * D_MODEL ** -0.5),
        'b_mod': nrm((DEPTH, 6 * D_MODEL), 0.01),
        'g_mix_pre': 1.0 + nrm((DEPTH, D_MODEL), 0.05),
        'g_mix_post': 1.0 + nrm((DEPTH, D_MODEL), 0.05),
        'g_ffn_pre': 1.0 + nrm((DEPTH, D_MODEL), 0.05),
        'g_ffn_post': 1.0 + nrm((DEPTH, D_MODEL), 0.05),
        'w_in_att': nrm((N_ATT, D_MODEL, ATT_IN), D_MODEL ** -0.5),
        'w_in_rec': nrm((N_REC, D_MODEL, REC_IN), D_MODEL ** -0.5),
        'w_out': nrm((DEPTH, MIX_W, D_MODEL), MIX_W ** -0.5),
        'a_sink': nrm((N_ATT, A_KV_HEADS, A_REP)),
        'b_rel_bias': nrm((N_ATT, B_HEADS, 2 * NA_ROWS - 1, 2 * NA_COLS - 1), 0.1),
        'c_conv': nrm((N_REC, CONV_W, 3 * C_W), CONV_W ** -0.5),
        'c_a_log': jnp.log(jax.random.uniform(nxt(), (N_REC, 2, C_HEADS), jnp.float32, 1.0, 16.0)),
        'c_dt_bias': dt + jnp.log(-jnp.expm1(-dt)),
        'c_norm': 1.0 + nrm((N_REC, HD), 0.05),
        'd_b_i': nrm((N_REC, 2, D_HEADS), 0.1),
        'd_b_f': jnp.linspace(3.0, 6.0, D_HEADS, dtype=jnp.float32) + nrm((N_REC, 2, D_HEADS), 0.1),
        'd_norm': 1.0 + nrm((N_REC, HD), 0.05),
        'w_up': nrm((DEPTH, D_MODEL, FF), D_MODEL ** -0.5),
        'w_down': nrm((DEPTH, FF, D_MODEL), FF ** -0.5),
    }


def reference(x_prompt, x_sample, cache_a_k, cache_a_v, cache_b_k, cache_b_v, state_c, state_d_c,
              state_d_n, state_d_m, c, c_ctx, w_mod, b_mod, g_mix_pre, g_mix_post, g_ffn_pre, g_ffn_post,
              w_in_att, w_in_rec, w_out, a_sink, b_rel_bias, c_conv, c_a_log, c_dt_bias, c_norm,
              d_b_i, d_b_f, d_norm, w_up, w_down):

    def ffn_sub(x, l, sh, sc, gt):
        h = rmsnorm(x, g_ffn_pre[l]) * (1 + sc) + sh
        return x + gt * rmsnorm(mlp(h, w_up[l], w_down[l]), g_ffn_post[l])

    x = x_prompt
    nb0 = x.shape[0]
    att_states = []
    rec_states = []
    for l in range(DEPTH):
        j = l // 2
        sh1, sc1, gt1, sh2, sc2, gt2 = adaln(c_ctx[None, :], w_mod[l], b_mod[l])
        h = rmsnorm(x, g_mix_pre[l]) * (1 + sc1) + sh1
        if l % 2 == 0:
            y, st = att_mixer_ctx(h, w_in_att[j], w_out[l], a_sink[j])
            att_states.append(st)
        else:
            z_c = jnp.zeros((nb0, 2, C_HEADS, HD, HD), jnp.float32)
            z_dc = jnp.zeros((nb0, 2, D_HEADS, HD, HD), jnp.float32)
            z_dn = jnp.zeros((nb0, 2, D_HEADS, HD), jnp.float32)
            z_dm = jnp.zeros((nb0, 2, D_HEADS), jnp.float32)
            y, st = rec_mixer(h, w_in_rec[j], w_out[l], c_conv[j], c_a_log[j], c_dt_bias[j], c_norm[j],
                              d_b_i[j], d_b_f[j], d_norm[j], z_c, z_dc, z_dn, z_dm)
            rec_states.append(st)
        x = x + gt1 * rmsnorm(y, g_mix_post[l])
        x = ffn_sub(x, l, sh2, sc2, gt2)
    y_prompt = x
    new_a_k = jnp.stack([s[0] for s in att_states], axis=1)
    new_a_v = jnp.stack([s[1] for s in att_states], axis=1)
    new_b_k = jnp.stack([s[2] for s in att_states], axis=1)
    new_b_v = jnp.stack([s[3] for s in att_states], axis=1)
    new_state_c = jnp.stack([s[0] for s in rec_states], axis=1)
    new_state_d_c = jnp.stack([s[1] for s in rec_states], axis=1)
    new_state_d_n = jnp.stack([s[2] for s in rec_states], axis=1)
    new_state_d_m = jnp.stack([s[3] for s in rec_states], axis=1)

    x = x_sample
    for l in range(DEPTH):
        j = l // 2
        sh1, sc1, gt1, sh2, sc2, gt2 = adaln(c, w_mod[l], b_mod[l])
        h = rmsnorm(x, g_mix_pre[l]) * (1 + sc1) + sh1
        if l % 2 == 0:
            y = att_mixer_lat(h, w_in_att[j], w_out[l], a_sink[j], b_rel_bias[j], cache_a_k[:, j],
                              cache_a_v[:, j], cache_b_k[:, j], cache_b_v[:, j])
        else:
            y, _ = rec_mixer(h, w_in_rec[j], w_out[l], c_conv[j], c_a_log[j], c_dt_bias[j], c_norm[j],
                             d_b_i[j], d_b_f[j], d_norm[j], state_c[:, j], state_d_c[:, j],
                             state_d_n[:, j], state_d_m[:, j])
        x = x + gt1 * rmsnorm(y, g_mix_post[l])
        x = ffn_sub(x, l, sh2, sc2, gt2)
    y_sample = x

    return (y_prompt, y_sample, new_a_k, new_a_v, new_b_k, new_b_v, new_state_c, new_state_d_c, new_state_d_n, new_state_d_m)
```

```python
import functools
import math

import numpy as np
import jax
import jax.numpy as jnp
from jax import lax
from jax.experimental import pallas as pl
from jax.experimental.pallas import tpu as pltpu

f32 = jnp.float32
bf16 = jnp.bfloat16
HI = lax.Precision.HIGHEST

D_MODEL = 2048
BATCH = 16
SEQ = 256
DEPTH = 4
DEC_BATCH = 8
DEC_SEQ = 2048
PAST_LEN = 512
GRID_W = 64
HD = 128
HEADS = 8
A_KV_HEADS = 2
A_REP = 4
FF = 4 * D_MODEL
WINDOW = 128
NA_ROWS = 8
NA_COLS = 16
CHUNK = 64
ROPE_BASE = 10000.0
EPS = 1e-6
N_ATT = 2
N_REC = 2
ATTN_SCALE = HD ** -0.5
GRID_ROWS = DEC_SEQ // GRID_W

N_CTX = BATCH * SEQ
N_LAT = DEC_BATCH * DEC_SEQ
N_ALL = N_CTX + N_LAT
MOD_ROWS = 16

ATT_SLABS = 36
REC_SLABS = 66
LANES = 128
VMEM_LIMIT = 56 * 1024 * 1024

NT = (((1,), (1,)), ((), ()))
TN_ = (((0,), (0,)), ((), ()))


def _cparams(sem):
    return pltpu.CompilerParams(dimension_semantics=sem, vmem_limit_bytes=VMEM_LIMIT)


def _sigmoid(x):
    return 1.0 / (1.0 + jnp.exp(-x))


def _softplus(x):
    e = jnp.exp(-jnp.abs(x))
    u = 1.0 + e
    log1p = jnp.where(u == 1.0, e, jnp.log(u) * (e / (u - 1.0)))
    return jnp.maximum(x, 0.0) + log1p


def _mod_row(tm):
    ctx_tiles = N_CTX // tm
    per_batch = DEC_SEQ // tm
    return lambda i: jnp.where(i < ctx_tiles, 0, 1 + (i - ctx_tiles) // per_batch)


def _adaln_body(c_ref, w_ref, b_ref, o_ref):
    c = c_ref[...]
    s = (c * _sigmoid(c)).astype(bf16)
    o_ref[0] = jnp.dot(s, w_ref[0].astype(bf16), preferred_element_type=f32) + b_ref[0]


def adaln_all(c_all, w_mod, b_mod):
    tn = 1024
    return pl.pallas_call(
        _adaln_body,
        out_shape=jax.ShapeDtypeStruct((DEPTH, MOD_ROWS, 6 * D_MODEL), f32),
        grid=(DEPTH, 6 * D_MODEL // tn),
        in_specs=[pl.BlockSpec((MOD_ROWS, D_MODEL), lambda l, j: (0, 0)),
                  pl.BlockSpec((1, D_MODEL, tn), lambda l, j: (l, 0, j)),
                  pl.BlockSpec((1, 1, tn), lambda l, j: (l, 0, j))],
        out_specs=pl.BlockSpec((1, MOD_ROWS, tn), lambda l, j: (l, 0, j)),
        compiler_params=_cparams(("arbitrary", "arbitrary")),
        name="adaln",
    )(c_all, w_mod, b_mod.reshape(DEPTH, 1, 6 * D_MODEL))


def _norm_mod_rows(x_ref, g_ref, m_ref, h_ref, tm, sub, sh_row, sc_row):
    g = g_ref[...]
    sc = 1.0 + m_ref[0, sc_row:sc_row + 1, :]
    sh = m_ref[0, sh_row:sh_row + 1, :]

    def body(r, carry):
        rows = pl.ds(pl.multiple_of(r * sub, sub), sub)
        x = x_ref[rows, :]
        ms = jnp.mean(x * x, axis=-1, keepdims=True)
        y = x * lax.rsqrt(ms + EPS) * g
        h_ref[rows, :] = (y * sc + sh).astype(bf16)
        return carry

    lax.fori_loop(0, tm // sub, body, 0)


def _post_norm_residual_rows(y_ref, x_ref, g_ref, m_ref, tm, sub, gate_row):
    g = g_ref[...]
    gt = m_ref[0, gate_row:gate_row + 1, :]

    def body(r, carry):
        rows = pl.ds(pl.multiple_of(r * sub, sub), sub)
        y = y_ref[rows, :]
        ms = jnp.mean(y * y, axis=-1, keepdims=True)
        y_ref[rows, :] = x_ref[rows, :] + gt * (y * lax.rsqrt(ms + EPS) * g)
        return carry

    lax.fori_loop(0, tm // sub, body, 0)


def _proj_body(x_ref, g_ref, m_ref, w_ref, o_ref, h_ref, *, tm, tn, sub):
    @pl.when(pl.program_id(1) == 0)
    def _():
        _norm_mod_rows(x_ref, g_ref, m_ref, h_ref, tm, sub, 0, 1)

    acc = jnp.dot(h_ref[...], w_ref[...], preferred_element_type=f32)
    for s in range(tn // LANES):
        o_ref[s] = acc[:, s * LANES:(s + 1) * LANES]


def norm_mod_project(x, g_pre, mod_l, w_bf16):
    tm, tn, sub = 1024, 768, 64
    ncols = w_bf16.shape[1]
    mrow = _mod_row(tm)
    return pl.pallas_call(
        functools.partial(_proj_body, tm=tm, tn=tn, sub=sub),
        out_shape=jax.ShapeDtypeStruct((ncols // LANES, N_ALL, LANES), f32),
        grid=(N_ALL // tm, ncols // tn),
        in_specs=[pl.BlockSpec((tm, D_MODEL), lambda i, j: (i, 0)),
                  pl.BlockSpec((1, D_MODEL), lambda i, j: (0, 0)),
                  pl.BlockSpec((1, 6, D_MODEL), lambda i, j: (mrow(i), 0, 0)),
                  pl.BlockSpec((D_MODEL, tn), lambda i, j: (0, j))],
        out_specs=pl.BlockSpec((tn // LANES, tm, LANES), lambda i, j: (j, i, 0)),
        scratch_shapes=[pltpu.VMEM((tm, D_MODEL), bf16)],
        compiler_params=_cparams(("arbitrary", "arbitrary")),
        name="norm_mod_project",
    )(x, g_pre.reshape(1, D_MODEL), mod_l, w_bf16)


def _outproj_body(o1_ref, o2_ref, w_ref, x_ref, g_ref, m_ref, out_ref, *, tm, sub):
    half = D_MODEL // 2
    out_ref[...] = (jnp.dot(o1_ref[...], w_ref[0:half, :], preferred_element_type=f32)
                    + jnp.dot(o2_ref[...], w_ref[half:D_MODEL, :], preferred_element_type=f32))
    _post_norm_residual_rows(out_ref, x_ref, g_ref, m_ref, tm, sub, 2)


def out_project_residual(o1, o2, w_bf16, x, g_post, mod_l):
    tm, sub = 512, 64
    half = D_MODEL // 2
    mrow = _mod_row(tm)
    return pl.pallas_call(
        functools.partial(_outproj_body, tm=tm, sub=sub),
        out_shape=jax.ShapeDtypeStruct((N_ALL, D_MODEL), f32),
        grid=(N_ALL // tm,),
        in_specs=[pl.BlockSpec((tm, half), lambda i: (i, 0)),
                  pl.BlockSpec((tm, half), lambda i: (i, 0)),
                  pl.BlockSpec((D_MODEL, D_MODEL), lambda i: (0, 0)),
                  pl.BlockSpec((tm, D_MODEL), lambda i: (i, 0)),
                  pl.BlockSpec((1, D_MODEL), lambda i: (0, 0)),
                  pl.BlockSpec((1, 6, D_MODEL), lambda i: (mrow(i), 0, 0))],
        out_specs=pl.BlockSpec((tm, D_MODEL), lambda i: (i, 0)),
        compiler_params=_cparams(("arbitrary",)),
        name="out_project_residual",
    )(o1, o2, w_bf16, x, g_post.reshape(1, D_MODEL), mod_l)


def _mlp_body(x_ref, gpre_ref, gpost_ref, m_ref, wu_ref, wd_ref, out_ref, h_ref, *, tm, sub, nj):
    j = pl.program_id(1)

    @pl.when(j == 0)
    def _():
        _norm_mod_rows(x_ref, gpre_ref, m_ref, h_ref, tm, sub, 3, 4)

    u = jnp.dot(h_ref[...], wu_ref[...], preferred_element_type=f32)
    u = jnp.maximum(u, 0.0)
    u = (u * u).astype(bf16)
    part = jnp.dot(u, wd_ref[...], preferred_element_type=f32)

    @pl.when(j == 0)
    def _():
        out_ref[...] = part

    @pl.when(j > 0)
    def _():
        out_ref[...] += part

    @pl.when(j == nj - 1)
    def _():
        _post_norm_residual_rows(out_ref, x_ref, gpost_ref, m_ref, tm, sub, 5)


def mlp_residual(x, g_pre, g_post, mod_l, wu_bf16, wd_bf16):
    tm, tf, sub = 512, 512, 64
    nj = FF // tf
    mrow = _mod_row(tm)
    return pl.pallas_call(
        functools.partial(_mlp_body, tm=tm, sub=sub, nj=nj),
        out_shape=jax.ShapeDtypeStruct((N_ALL, D_MODEL), f32),
        grid=(N_ALL // tm, nj),
        in_specs=[pl.BlockSpec((tm, D_MODEL), lambda i, j: (i, 0)),
                  pl.BlockSpec((1, D_MODEL), lambda i, j: (0, 0)),
                  pl.BlockSpec((1, D_MODEL), lambda i, j: (0, 0)),
                  pl.BlockSpec((1, 6, D_MODEL), lambda i, j: (mrow(i), 0, 0)),
                  pl.BlockSpec((D_MODEL, tf), lambda i, j: (0, j)),
                  pl.BlockSpec((tf, D_MODEL), lambda i, j: (j, 0))],
        out_specs=pl.BlockSpec((tm, D_MODEL), lambda i, j: (i, 0)),
        scratch_shapes=[pltpu.VMEM((tm, D_MODEL), bf16)],
        compiler_params=_cparams(("arbitrary", "arbitrary")),
        name="mlp_residual",
    )(x, g_pre.reshape(1, D_MODEL), g_post.reshape(1, D_MODEL), mod_l, wu_bf16, wd_bf16)


def _rope_tables():
    nf = HD // 4
    t = jnp.arange(DEC_SEQ)
    row = (t // GRID_W).astype(f32)
    col = (t % GRID_W).astype(f32)
    inv = jnp.exp(-math.log(ROPE_BASE) * jnp.arange(nf, dtype=f32) / nf)
    lane = np.arange(HD)
    first_half = (lane % (2 * nf)) < nf
    pos = jnp.where(jnp.asarray(lane < HD // 2)[None, :], row[:, None], col[:, None])
    ang = pos * inv[jnp.asarray(lane % nf)][None, :]
    cos, sin = jnp.cos(ang), jnp.sin(ang)
    fh = jnp.asarray(first_half)[None, :]
    sin_a = jnp.where(fh, -sin, 0.0)
    sin_b = jnp.where(fh, 0.0, sin)

    def tab(a, ident):
        a = a.reshape(2, DEC_SEQ // 2, HD)
        return jnp.concatenate([jnp.full((1, DEC_SEQ // 2, HD), ident, f32), a], axis=0)

    return tab(cos, 1.0), tab(sin_a, 0.0), tab(sin_b, 0.0)


def _prep_att_body(p_ref, cos_ref, sa_ref, sb_ref, o_ref, *, tm):
    s = pl.program_id(0)
    i = pl.program_id(1)
    ctx_tiles = N_CTX // tm
    roped = jnp.logical_and(jnp.logical_or(s < HEADS, s >= 32) & (s < 34), i >= ctx_tiles)
    t = jnp.where(roped, 1 + (i - ctx_tiles) % 2, 0)
    scale = jnp.where(s < 2 * HEADS, ATTN_SCALE, 1.0).astype(f32)
    x = p_ref[0]
    y = x * cos_ref[t] + pltpu.roll(x, 96, 1) * sa_ref[t] + pltpu.roll(x, 32, 1) * sb_ref[t]
    o_ref[0] = (y * scale).astype(bf16)


def prep_att(p_att, tables):
    tm = DEC_SEQ // 2
    cos, sa, sb = tables
    tspec = pl.BlockSpec((3, tm, HD), lambda s, i: (0, 0, 0))
    return pl.pallas_call(
        functools.partial(_prep_att_body, tm=tm),
        out_shape=jax.ShapeDtypeStruct((ATT_SLABS, N_ALL, HD), bf16),
        grid=(ATT_SLABS, N_ALL // tm),
        in_specs=[pl.BlockSpec((1, tm, HD), lambda s, i: (s, i, 0)), tspec, tspec, tspec],
        out_specs=pl.BlockSpec((1, tm, HD), lambda s, i: (s, i, 0)),
        compiler_params=_cparams(("arbitrary", "arbitrary")),
        name="prep_att",
    )(p_att, cos, sa, sb)


def _ctx_attn_body(*refs, reps, use_sink):
    if use_sink:
        sink_ref, q_ref, k_ref, v_ref, o_ref = refs
    else:
        q_ref, k_ref, v_ref, o_ref = refs
    g = pl.program_id(1)
    k = k_ref[0]
    v = v_ref[0]
    for r in range(reps):
        s = lax.dot_general(q_ref[r], k, NT, preferred_element_type=f32)
        m = jnp.max(s, axis=-1, keepdims=True)
        if use_sink:
            sk = sink_ref[g, r]
            m = jnp.maximum(m, sk)
        p = jnp.exp(s - m)
        den = jnp.sum(p, axis=-1, keepdims=True)
        if use_sink:
            den = den + jnp.exp(sk - m)
        o = jnp.dot(p.astype(bf16), v, preferred_element_type=f32) / den
        o_ref[:, r * HD:(r + 1) * HD] = o.astype(bf16)


def ctx_attention_a(pb, sink):
    return pl.pallas_call(
        functools.partial(_ctx_attn_body, reps=A_REP, use_sink=True),
        out_shape=jax.ShapeDtypeStruct((N_CTX, HEADS * HD), bf16),
        grid=(BATCH, A_KV_HEADS),
        in_specs=[pl.BlockSpec(memory_space=pltpu.SMEM),
                  pl.BlockSpec((A_REP, SEQ, HD), lambda b, g: (g, b, 0)),
                  pl.BlockSpec((1, SEQ, HD), lambda b, g: (32 + g, b, 0)),
                  pl.BlockSpec((1, SEQ, HD), lambda b, g: (34 + g, b, 0))],
        out_specs=pl.BlockSpec((SEQ, A_REP * HD), lambda b, g: (b, g)),
        compiler_params=_cparams(("arbitrary", "arbitrary")),
        name="ctx_attention_a",
    )(sink, pb, pb, pb)


def ctx_attention_b(pb):
    return pl.pallas_call(
        functools.partial(_ctx_attn_body, reps=1, use_sink=False),
        out_shape=jax.ShapeDtypeStruct((N_CTX, HEADS * HD), bf16),
        grid=(BATCH, HEADS),
        in_specs=[pl.BlockSpec((1, SEQ, HD), lambda b, h: (8 + h, b, 0)),
                  pl.BlockSpec((1, SEQ, HD), lambda b, h: (16 + h, b, 0)),
                  pl.BlockSpec((1, SEQ, HD), lambda b, h: (24 + h, b, 0))],
        out_specs=pl.BlockSpec((SEQ, HD), lambda b, h: (b, h)),
        compiler_params=_cparams(("arbitrary", "arbitrary")),
        name="ctx_attention_b",
    )(pb, pb, pb)


def _window_body(sink_ref, q_ref, k0_ref, k1_ref, k2_ref, v0_ref, v1_ref, v2_ref, kc_ref, vc_ref, o_ref, *, nb):
    g = pl.program_id(1)
    n = pl.program_id(2)
    rows = A_REP * WINDOW
    q = q_ref[...].reshape(rows, HD)
    kc = kc_ref[0, 0].astype(bf16)
    vc = vc_ref[0, 0].astype(bf16)
    s0 = lax.dot_general(q, k0_ref[0], NT, preferred_element_type=f32)
    s1 = lax.dot_general(q, k1_ref[0], NT, preferred_element_type=f32)
    s2 = lax.dot_general(q, k2_ref[0], NT, preferred_element_type=f32)
    sc = lax.dot_general(q, kc, NT, preferred_element_type=f32)
    qi = lax.broadcasted_iota(jnp.int32, (rows, WINDOW), 0) % WINDOW
    kj = lax.broadcasted_iota(jnp.int32, (rows, WINDOW), 1)
    neg = -jnp.inf
    s0 = jnp.where(kj >= qi, s0, neg)
    s0 = jnp.where(n > 0, s0, neg)
    s2 = jnp.where(kj <= qi, s2, neg)
    s2 = jnp.where(n < nb - 1, s2, neg)
    rep = lax.broadcasted_iota(jnp.int32, (rows, 1), 0) // WINDOW
    sk = jnp.zeros((rows, 1), f32)
    for r in range(A_REP):
        sk = jnp.where(rep == r, sink_ref[g, r], sk)
    m = jnp.maximum(jnp.maximum(jnp.max(s0, axis=-1, keepdims=True), jnp.max(s1, axis=-1, keepdims=True)),
                    jnp.maximum(jnp.max(s2, axis=-1, keepdims=True), jnp.max(sc, axis=-1, keepdims=True)))
    m = jnp.maximum(m, sk)
    p0 = jnp.exp(s0 - m)
    p1 = jnp.exp(s1 - m)
    p2 = jnp.exp(s2 - m)
    pc = jnp.exp(sc - m)
    den = (jnp.sum(p0, axis=-1, keepdims=True) + jnp.sum(p1, axis=-1, keepdims=True)
           + jnp.sum(p2, axis=-1, keepdims=True) + jnp.sum(pc, axis=-1, keepdims=True) + jnp.exp(sk - m))
    o = (jnp.dot(p0.astype(bf16), v0_ref[0], preferred_element_type=f32)
         + jnp.dot(p1.astype(bf16), v1_ref[0], preferred_element_type=f32)
         + jnp.dot(p2.astype(bf16), v2_ref[0], preferred_element_type=f32)
         + jnp.dot(pc.astype(bf16), vc, preferred_element_type=f32)) / den
    for r in range(A_REP):
        o_ref[:, r * HD:(r + 1) * HD] = o[r * WINDOW:(r + 1) * WINDOW].astype(bf16)


def window_attention_lat(pb, sink, ka_c, va_c, j):
    nb = DEC_SEQ // WINDOW
    base = N_CTX // WINDOW

    def qrow(b, n):
        return base + b * nb + n

    def kspec(slab0, shift):
        return pl.BlockSpec((1, WINDOW, HD),
                            lambda b, g, n: (slab0 + g, qrow(b, jnp.clip(n + shift, 0, nb - 1)), 0))

    cspec = pl.BlockSpec((1, 1, PAST_LEN, HD), lambda b, g, n: (b, j, 0, g))
    return pl.pallas_call(
        functools.partial(_window_body, nb=nb),
        out_shape=jax.ShapeDtypeStruct((N_LAT, HEADS * HD), bf16),
        grid=(DEC_BATCH, A_KV_HEADS, nb),
        in_specs=[pl.BlockSpec(memory_space=pltpu.SMEM),
                  pl.BlockSpec((A_REP, WINDOW, HD), lambda b, g, n: (g, qrow(b, n), 0)),
                  kspec(32, -1), kspec(32, 0), kspec(32, 1),
                  kspec(34, -1), kspec(34, 0), kspec(34, 1),
                  cspec, cspec],
        out_specs=pl.BlockSpec((WINDOW, A_REP * HD), lambda b, g, n: (b * nb + n, g)),
        compiler_params=_cparams(("arbitrary", "arbitrary", "arbitrary")),
        name="window_attention",
    )(sink, pb, pb, pb, pb, pb, pb, pb, ka_c, va_c)


NA_QROWS = 2
NA_WROWS = NA_QROWS + NA_ROWS - 1
NA_WIN_MAX = GRID_ROWS - NA_WROWS
NA_OFFS = (0, -2, -4, -5, -7)
NA_Q = NA_QROWS * GRID_W
NA_K = NA_WROWS * GRID_W


def _na_bias_tables(rel_bias):
    r0_of = {0: 0, -2: 2, -4: 4, -5: 28, -7: 30}
    dr_l, dc_l, ok_l = [], [], []
    a = np.arange(NA_QROWS)[:, None, None, None]
    cq = np.arange(GRID_W)[None, :, None, None]
    w = np.arange(NA_WROWS)[None, None, :, None]
    ck = np.arange(GRID_W)[None, None, None, :]
    for off in NA_OFFS:
        r0 = r0_of[off]
        ws = int(np.clip(r0 - NA_ROWS // 2, 0, NA_WIN_MAX))
        assert ws - r0 == off
        r = r0 + a
        rs = np.clip(r - NA_ROWS // 2, 0, GRID_ROWS - NA_ROWS)
        krow = ws + w
        wstart = np.clip(cq - NA_COLS // 2, 0, GRID_W - NA_COLS)
        ok = (krow >= rs) & (krow < rs + NA_ROWS) & (ck >= wstart) & (ck < wstart + NA_COLS)
        dr = np.clip(krow - r + NA_ROWS - 1, 0, 2 * NA_ROWS - 2)
        dc = np.clip(ck - cq + NA_COLS - 1, 0, 2 * NA_COLS - 2)
        shape = (NA_QROWS, GRID_W, NA_WROWS, GRID_W)
        dr_l.append(np.broadcast_to(dr, shape).reshape(NA_Q, NA_K))
        dc_l.append(np.broadcast_to(dc, shape).reshape(NA_Q, NA_K))
        ok_l.append(np.broadcast_to(ok, shape).reshape(NA_Q, NA_K))
    dr = jnp.asarray(np.stack(dr_l))
    dc = jnp.asarray(np.stack(dc_l))
    ok = jnp.asarray(np.stack(ok_l))
    vals = rel_bias[:, dr, dc]
    return jnp.transpose(jnp.where(ok[None], vals, -1e30), (1, 0, 2, 3)).astype(f32)


def _na_body(bias_ref, q_ref, k_ref, v_ref, kc_ref, vc_ref, o_ref, kcb_ref, vcb_ref):
    rb = pl.program_id(1)

    @pl.when(rb == 0)
    def _():
        kcb_ref[...] = kc_ref[0, 0].astype(bf16)
        vcb_ref[...] = vc_ref[0, 0].astype(bf16)

    ws = jnp.clip(NA_QROWS * rb - NA_ROWS // 2, 0, NA_WIN_MAX)
    win = pl.ds(pl.multiple_of(ws * GRID_W, GRID_W), NA_K)
    for h in range(HEADS):
        q = q_ref[h]
        lanes = slice(h * HD, (h + 1) * HD)
        sl = lax.dot_general(q, k_ref[h, win, :], NT, preferred_element_type=f32) + bias_ref[0, h]
        sc = lax.dot_general(q, kcb_ref[:, lanes], NT, preferred_element_type=f32)
        m = jnp.maximum(jnp.max(sl, axis=-1, keepdims=True), jnp.max(sc, axis=-1, keepdims=True))
        p_l = jnp.exp(sl - m)
        p_c = jnp.exp(sc - m)
        den = jnp.sum(p_l, axis=-1, keepdims=True) + jnp.sum(p_c, axis=-1, keepdims=True)
        o = (jnp.dot(p_l.astype(bf16), v_ref[h, win, :], preferred_element_type=f32)
             + jnp.dot(p_c.astype(bf16), vcb_ref[:, lanes], preferred_element_type=f32)) / den
        o_ref[:, lanes] = o.astype(bf16)


def neighbourhood_attention_lat(pb, bias_tab, kb_c, vb_c, j):
    nrb = GRID_ROWS // NA_QROWS
    base = N_CTX // NA_Q
    per_b = DEC_SEQ // NA_Q
    seq_base = N_CTX // DEC_SEQ

    def tbl(rb):
        return jnp.where(rb < 2, rb, jnp.where(rb < nrb - 2, 2, rb - (nrb - 5)))

    cspec = pl.BlockSpec((1, 1, PAST_LEN, HEADS * HD), lambda b, rb: (b, j, 0, 0))
    return pl.pallas_call(
        _na_body,
        out_shape=jax.ShapeDtypeStruct((N_LAT, HEADS * HD), bf16),
        grid=(DEC_BATCH, nrb),
        in_specs=[pl.BlockSpec((1, HEADS, NA_Q, NA_K), lambda b, rb: (tbl(rb), 0, 0, 0)),
                  pl.BlockSpec((HEADS, NA_Q, HD), lambda b, rb: (1, base + b * per_b + rb, 0)),
                  pl.BlockSpec((HEADS, DEC_SEQ, HD), lambda b, rb: (2, seq_base + b, 0)),
                  pl.BlockSpec((HEADS, DEC_SEQ, HD), lambda b, rb: (3, seq_base + b, 0)),
                  cspec, cspec],
        out_specs=pl.BlockSpec((NA_Q, HEADS * HD), lambda b, rb: (b * per_b + rb, 0)),
        scratch_shapes=[pltpu.VMEM((PAST_LEN, HEADS * HD), bf16), pltpu.VMEM((PAST_LEN, HEADS * HD), bf16)],
        compiler_params=_cparams(("arbitrary", "arbitrary")),
        name="neighbourhood_attention",
    )(bias_tab, pb, pb, pb, kb_c, vb_c)


def _gates_body(p_ref, par_ref, o_ref):
    x = p_ref[0]
    lane = lax.broadcasted_iota(jnp.int32, x.shape, 1)
    z = x + par_ref[1:2, :]
    sig = _sigmoid(z)
    g = par_ref[0:1, :] * _softplus(z)
    lsig = -_softplus(-z)
    o_ref[...] = jnp.where(lane < 16, sig, jnp.where(lane < 32, g, jnp.where(lane < 48, z, lsig)))


def gate_activations(p_rec, a_log, dt_bias, b_i, b_f):
    tm = 1024
    zeros16 = jnp.zeros((16,), f32)
    neg_a = jnp.concatenate([zeros16, -jnp.exp(a_log.reshape(16).astype(f32)), zeros16, zeros16, jnp.zeros((64,), f32)])
    bias = jnp.concatenate([zeros16, dt_bias.reshape(16).astype(f32), b_i.reshape(16).astype(f32),
                            b_f.reshape(16).astype(f32), jnp.zeros((64,), f32)])
    par = jnp.concatenate([neg_a[None], bias[None], jnp.zeros((6, LANES), f32)], axis=0)
    return pl.pallas_call(
        _gates_body,
        out_shape=jax.ShapeDtypeStruct((N_ALL, LANES), f32),
        grid=(N_ALL // tm,),
        in_specs=[pl.BlockSpec((1, tm, LANES), lambda i: (64, i, 0)),
                  pl.BlockSpec((8, LANES), lambda i: (0, 0))],
        out_specs=pl.BlockSpec((tm, LANES), lambda i: (i, 0)),
        compiler_params=_cparams(("arbitrary",)),
        name="gate_activations",
    )(p_rec, par)


def _tri(reverse, strict):
    ii = lax.broadcasted_iota(jnp.int32, (CHUNK, CHUNK), 0)
    jj = lax.broadcasted_iota(jnp.int32, (CHUNK, CHUNK), 1)
    if reverse:
        return (jj > ii) if strict else (jj >= ii)
    return (jj < ii) if strict else (jj <= ii)


def _cumsum_row(x_r, reverse):
    aa = lax.broadcasted_iota(jnp.int32, (CHUNK, CHUNK), 0)
    bb = lax.broadcasted_iota(jnp.int32, (CHUNK, CHUNK), 1)
    u = jnp.where((aa >= bb) if reverse else (aa <= bb), 1.0, 0.0).astype(f32)
    x8 = jnp.broadcast_to(x_r, (8, CHUNK))
    return jnp.dot(x8, u, precision=HI, preferred_element_type=f32)[0:1]


def _rows_to_cols(r0, r1):
    ri = lax.broadcasted_iota(jnp.int32, (8, CHUNK), 0)
    x = jnp.where(ri == 0, jnp.broadcast_to(r0, (8, CHUNK)), jnp.where(ri == 1, jnp.broadcast_to(r1, (8, CHUNK)), 0.0))
    eye = jnp.where(_tri(False, False) & _tri(True, False), 1.0, 0.0).astype(f32)
    cols = lax.dot_general(eye, x, NT, precision=HI, preferred_element_type=f32)
    return cols[:, 0:1], cols[:, 1:2]


def _chunk_rows(c):
    return pl.ds(pl.multiple_of(c * CHUNK, CHUNK), CHUNK)


def _conv_silu(x_ref, cw_ref, T):
    x = x_ref[0]
    w = cw_ref[0]
    row = lax.broadcasted_iota(jnp.int32, (T, HD), 0)
    xp = jnp.where(row == 0, 0.0, pltpu.roll(x, 1, 0))
    xn = jnp.where(row == T - 1, 0.0, pltpu.roll(x, T - 1, 0))
    y = w[0:1] * xp + w[1:2] * x + w[2:3] * xn
    return y * _sigmoid(y)


def _delta_body(xq_ref, xk_ref, xv_ref, xz_ref, cwq_ref, cwk_ref, cwv_ref, bf_ref, bb_ref, gf_ref, gb_ref,
                s0_ref, cn_ref, o_ref, sout_ref,
                q_s, k_s, v_s, u_s, w_s, qg_s, kd_s, at_s, eg_s, o_s, *, T):
    nc = T // CHUNK
    q = _conv_silu(xq_ref, cwq_ref, T)
    q_s[...] = q * lax.rsqrt(jnp.sum(q * q, axis=-1, keepdims=True) + EPS) * ATTN_SCALE
    k = _conv_silu(xk_ref, cwk_ref, T)
    k_s[...] = k * lax.rsqrt(jnp.sum(k * k, axis=-1, keepdims=True) + EPS)
    v_s[...] = _conv_silu(xv_ref, cwv_ref, T)

    eye = jnp.where(_tri(False, False) & _tri(True, False), 1.0, 0.0).astype(f32)

    def prep(c, d, beta_ref, g_ref):
        reverse = d == 1
        rows = _chunk_rows(c)
        q = q_s[rows, :]
        k = k_s[rows, :]
        v = v_s[rows, :]
        gcum_r = _cumsum_row(g_ref[0, 0, pl.ds(c, 1), :], reverse)
        gcum_c, beta_c = _rows_to_cols(gcum_r, beta_ref[0, 0, pl.ds(c, 1), :])
        decay = jnp.exp(jnp.where(_tri(reverse, False), gcum_c - gcum_r, -jnp.inf))
        kb = k * beta_c
        a = jnp.where(_tri(reverse, True),
                      lax.dot_general(kb, k, NT, precision=HI, preferred_element_type=f32) * decay, 0.0)
        m = -a
        inv = eye + m
        for _ in range(5):
            m = jnp.dot(m, m, precision=HI, preferred_element_type=f32)
            inv = inv + jnp.dot(inv, m, precision=HI, preferred_element_type=f32)
        eg_c = jnp.exp(gcum_c)
        u_s[d, rows, :] = jnp.dot(inv, v * beta_c, precision=HI, preferred_element_type=f32)
        w_s[d, rows, :] = jnp.dot(inv, kb * eg_c, precision=HI, preferred_element_type=f32)
        qg_s[d, rows, :] = q * eg_c
        at_s[d, rows, :] = lax.dot_general(q, k, NT, precision=HI, preferred_element_type=f32) * decay
        glast = gcum_r[:, 0:1] if reverse else gcum_r[:, CHUNK - 1:CHUNK]
        kd_s[d, rows, :] = k * jnp.exp(glast - gcum_c)
        eg_s[d, pl.ds(pl.multiple_of(c * 8, 8), 8), :] = jnp.broadcast_to(jnp.exp(glast), (8, HD))

    def prep_step(c, carry):
        prep(c, 0, bf_ref, gf_ref)
        prep(c, 1, bb_ref, gb_ref)
        return carry

    lax.fori_loop(0, nc, prep_step, 0)

    def rec(c, d, s):
        rows = _chunk_rows(c)
        vn = u_s[d, rows, :] - jnp.dot(w_s[d, rows, :], s, precision=HI, preferred_element_type=f32)
        o_s[d, rows, :] = (jnp.dot(qg_s[d, rows, :], s, precision=HI, preferred_element_type=f32)
                           + jnp.dot(at_s[d, rows, :], vn, precision=HI, preferred_element_type=f32))
        eg = eg_s[d, pl.ds(pl.multiple_of(c * 8, 8), 8), :][0:1]
        return s * eg + lax.dot_general(kd_s[d, rows, :], vn, TN_, precision=HI, preferred_element_type=f32)

    def rec_step(i, carry):
        sf, sb = carry
        return rec(i, 0, sf), rec(nc - 1 - i, 1, sb)

    sf, sb = lax.fori_loop(0, nc, rec_step, (s0_ref[0, 0, 0], s0_ref[0, 1, 0]))
    sout_ref[0, 0, 0] = sf
    sout_ref[0, 1, 0] = sb

    o = o_s[0] + o_s[1]
    o = o * lax.rsqrt(jnp.mean(o * o, axis=-1, keepdims=True) + EPS) * cn_ref[...]
    z = xz_ref[0]
    o_ref[...] = (o * (z * _sigmoid(z))).astype(bf16)


def delta_mixer(p_rec, gates_t, conv_w, c_norm, s0, T, row0):
    B = s0.shape[0]
    nc = T // CHUNK
    rb0 = row0 // T

    def xspec(slab0):
        return pl.BlockSpec((1, T, HD), lambda b, h: (slab0 + h, rb0 + b, 0))

    def cwspec(slab0):
        return pl.BlockSpec((1, 3, HD), lambda b, h: (slab0 + h, 0, 0))

    def gspec(col0):
        return pl.BlockSpec((1, 1, nc, CHUNK), lambda b, h: (b, col0 + h, 0, 0))

    sspec = pl.BlockSpec((1, 2, 1, HD, HD), lambda b, h: (b, 0, h, 0, 0))
    big = pltpu.VMEM((2, T, HD), f32)
    o, sout = pl.pallas_call(
        functools.partial(_delta_body, T=T),
        out_shape=(jax.ShapeDtypeStruct((B * T, HEADS * HD), bf16),
                   jax.ShapeDtypeStruct((B, 2, HEADS, HD, HD), f32)),
        grid=(B, HEADS),
        in_specs=[xspec(0), xspec(8), xspec(16), xspec(24), cwspec(0), cwspec(8), cwspec(16),
                  gspec(0), gspec(8), gspec(16), gspec(24), sspec,
                  pl.BlockSpec((1, HD), lambda b, h: (0, 0))],
        out_specs=(pl.BlockSpec((T, HD), lambda b, h: (b, h)), sspec),
        scratch_shapes=[pltpu.VMEM((T, HD), f32), pltpu.VMEM((T, HD), f32), pltpu.VMEM((T, HD), f32),
                        big, big, big, big, pltpu.VMEM((2, T, CHUNK), f32),
                        pltpu.VMEM((2, nc * 8, HD), f32), big],
        compiler_params=_cparams(("arbitrary", "arbitrary")),
        name="delta_mixer",
    )(p_rec, p_rec, p_rec, p_rec, conv_w, conv_w, conv_w, gates_t, gates_t, gates_t, gates_t, s0,
      c_norm.reshape(1, HD))
    return o, sout


def _mlstm_body(m0_ref, xq_ref, xk_ref, xv_ref, xo_ref, if_ref, ib_ref, ff_ref, fb_ref, c0_ref, n0_ref, dn_ref,
                o_ref, cout_ref, nout_ref, mout_ref,
                qk_s, kw_s, col_s, ds_s, h_s, *, T):
    nc = T // CHUNK
    b_id = pl.program_id(0)
    h_id = pl.program_id(1)

    def prep(c, d, i_ref, f_ref, m):
        reverse = d == 1
        last = 0 if reverse else CHUNK - 1
        rows = _chunk_rows(c)
        q = xq_ref[0, rows, :] * ATTN_SCALE
        k = xk_ref[0, rows, :]
        i_r = i_ref[0, 0, pl.ds(c, 1), :]
        b_r = _cumsum_row(f_ref[0, 0, pl.ds(c, 1), :], reverse)
        b_c, i_c = _rows_to_cols(b_r, i_r)
        dmat = jnp.where(_tri(reverse, False), b_c - b_r + i_r, -jnp.inf)
        inter = b_c + m
        mt = jnp.maximum(inter, jnp.max(dmat, axis=-1, keepdims=True))
        w_inter = jnp.exp(inter - mt)
        qkw = lax.dot_general(q, k, NT, precision=HI, preferred_element_type=f32) * jnp.exp(dmat - mt)
        rsum = jnp.sum(qkw, axis=-1, keepdims=True)
        m_new = mt[last:last + 1, :]
        b_last = b_r[:, last:last + 1]
        qk_s[d, rows, :] = qkw
        kw_s[d, rows, :] = k * jnp.exp(b_last - b_c + i_c - m_new)
        lane = lax.broadcasted_iota(jnp.int32, (CHUNK, HD), 1)
        col_s[d, rows, :] = jnp.where(lane == 0, w_inter, jnp.where(lane == 1, rsum, jnp.exp(-mt)))
        ds_s[d, pl.ds(pl.multiple_of(c * 8, 8), 8), :] = jnp.broadcast_to(jnp.exp(b_last + m - m_new), (8, HD))
        return m_new

    def prep_step(i, carry):
        mf, mb = carry
        return prep(i, 0, if_ref, ff_ref, mf), prep(nc - 1 - i, 1, ib_ref, fb_ref, mb)

    m_init = (jnp.full((1, 1), m0_ref[b_id, 0, h_id], f32), jnp.full((1, 1), m0_ref[b_id, 1, h_id], f32))
    mf, mb = lax.fori_loop(0, nc, prep_step, m_init)

    def rec(c, d, cm, n):
        rows = _chunk_rows(c)
        q = xq_ref[0, rows, :] * ATTN_SCALE
        v = xv_ref[0, rows, :]
        cols = col_s[d, rows, :]
        w_inter = cols[:, 0:1]
        kw = kw_s[d, rows, :]
        num = (w_inter * jnp.dot(q, cm, precision=HI, preferred_element_type=f32)
               + jnp.dot(qk_s[d, rows, :], v, precision=HI, preferred_element_type=f32))
        den = w_inter * jnp.sum(q * n, axis=-1, keepdims=True) + cols[:, 1:2]
        h_s[d, rows, :] = num / jnp.maximum(jnp.abs(den), cols[:, 2:3])
        ds = ds_s[d, pl.ds(pl.multiple_of(c * 8, 8), 8), :][0:1]
        cm = ds * cm + lax.dot_general(kw, v, TN_, precision=HI, preferred_element_type=f32)
        n = ds * n + jnp.sum(kw, axis=0, keepdims=True)
        return cm, n

    def rec_step(i, carry):
        cf, nf, cb, nb = carry
        cf, nf = rec(i, 0, cf, nf)
        cb, nb = rec(nc - 1 - i, 1, cb, nb)
        return cf, nf, cb, nb

    cf, nf, cb, nb = lax.fori_loop(0, nc, rec_step,
                                   (c0_ref[0, 0, 0], n0_ref[0, 0, 0], c0_ref[0, 1, 0], n0_ref[0, 1, 0]))
    cout_ref[0, 0, 0] = cf
    cout_ref[0, 1, 0] = cb
    nout_ref[0, 0, 0] = nf
    nout_ref[0, 1, 0] = nb
    mout_ref[0, 0, 0] = jnp.broadcast_to(mf, (1, HD))
    mout_ref[0, 1, 0] = jnp.broadcast_to(mb, (1, HD))

    h = h_s[0] + h_s[1]
    h = h * lax.rsqrt(jnp.mean(h * h, axis=-1, keepdims=True) + EPS) * dn_ref[...]
    o_ref[...] = (h * _sigmoid(xo_ref[0])).astype(bf16)


def mlstm_mixer(p_rec, gates_t, d_norm, c0, n0, m0, T, row0):
    B = c0.shape[0]
    nc = T // CHUNK
    rb0 = row0 // T

    def xspec(slab0):
        return pl.BlockSpec((1, T, HD), lambda b, h: (slab0 + h, rb0 + b, 0))

    def gspec(col0):
        return pl.BlockSpec((1, 1, nc, CHUNK), lambda b, h: (b, col0 + h, 0, 0))

    cspec = pl.BlockSpec((1, 2, 1, HD, HD), lambda b, h: (b, 0, h, 0, 0))
    vspec = pl.BlockSpec((1, 2, 1, 1, HD), lambda b, h: (b, 0, h, 0, 0))
    big = pltpu.VMEM((2, T, HD), f32)
    o, cout, nout, mout = pl.pallas_call(
        functools.partial(_mlstm_body, T=T),
        out_shape=(jax.ShapeDtypeStruct((B * T, HEADS * HD), bf16),
                   jax.ShapeDtypeStruct((B, 2, HEADS, HD, HD), f32),
                   jax.ShapeDtypeStruct((B, 2, HEADS, 1, HD), f32),
                   jax.ShapeDtypeStruct((B, 2, HEADS, 1, HD), f32)),
        grid=(B, HEADS),
        in_specs=[pl.BlockSpec(memory_space=pltpu.SMEM),
                  xspec(32), xspec(40), xspec(48), xspec(56),
                  gspec(32), gspec(40), gspec(48), gspec(56), cspec, vspec,
                  pl.BlockSpec((1, HD), lambda b, h: (0, 0))],
        out_specs=(pl.BlockSpec((T, HD), lambda b, h: (b, h)), cspec, vspec, vspec),
        scratch_shapes=[pltpu.VMEM((2, T, CHUNK), f32), big, big, pltpu.VMEM((2, nc * 8, HD), f32), big],
        compiler_params=_cparams(("arbitrary", "arbitrary")),
        name="mlstm_mixer",
    )(m0, p_rec, p_rec, p_rec, p_rec, gates_t, gates_t, gates_t, gates_t, c0, n0.reshape(B, 2, HEADS, 1, HD),
      d_norm.reshape(1, HD))
    return o, cout, nout, mout


_ATT_PERM = np.concatenate([np.arange(0, 1024), np.arange(1536, 4608), np.arange(1024, 1536)])
_REC_PERM = np.concatenate([np.arange(0, 4096), np.arange(4128, 8224), np.arange(4096, 4128), np.arange(8224, 8256)])


def _gates_rows(gates, row0, B, T):
    g = gates[row0:row0 + B * T, :64].reshape(B, T // CHUNK, CHUNK, 64)
    return jnp.transpose(g, (0, 3, 1, 2))


def kernel(x_prompt, x_sample, cache_a_k, cache_a_v, cache_b_k, cache_b_v, state_c, state_d_c, state_d_n, state_d_m, c, c_ctx, w_mod, b_mod, g_mix_pre, g_mix_post, g_ffn_pre, g_ffn_post, w_in_att, w_in_rec, w_out, a_sink, b_rel_bias, c_conv, c_a_log, c_dt_bias, c_norm, d_b_i, d_b_f, d_norm, w_up, w_down):
    x = jnp.concatenate([x_prompt.reshape(N_CTX, D_MODEL), x_sample.reshape(N_LAT, D_MODEL)], axis=0)
    c_all = jnp.concatenate([c_ctx[None, :], c, jnp.zeros((MOD_ROWS - 1 - DEC_BATCH, D_MODEL), f32)], axis=0)
    mods = adaln_all(c_all, w_mod, b_mod).reshape(DEPTH, MOD_ROWS, 6, D_MODEL)
    tables = _rope_tables()

    ka_c = cache_a_k.reshape(DEC_BATCH, N_ATT, PAST_LEN, A_KV_HEADS * HD)
    va_c = cache_a_v.reshape(DEC_BATCH, N_ATT, PAST_LEN, A_KV_HEADS * HD)
    kb_c = cache_b_k.reshape(DEC_BATCH, N_ATT, PAST_LEN, HEADS * HD)
    vb_c = cache_b_v.reshape(DEC_BATCH, N_ATT, PAST_LEN, HEADS * HD)

    att_kv = []
    rec_states = []
    for l in range(DEPTH):
        j = l // 2
        mod_l = mods[l]
        if l % 2 == 0:
            w_in = w_in_att[j][:, _ATT_PERM].astype(bf16)
            p = norm_mod_project(x, g_mix_pre[l], mod_l, w_in)
            att_kv.append(p[16:36, :N_CTX])
            pb = prep_att(p, tables)
            sink = a_sink[j].astype(f32)
            o1 = jnp.concatenate([ctx_attention_a(pb, sink), window_attention_lat(pb, sink, ka_c, va_c, j)], axis=0)
            bias_tab = _na_bias_tables(b_rel_bias[j].astype(f32))
            o2 = jnp.concatenate([ctx_attention_b(pb), neighbourhood_attention_lat(pb, bias_tab, kb_c, vb_c, j)], axis=0)
        else:
            w_in = jnp.pad(w_in_rec[j][:, _REC_PERM], ((0, 0), (0, REC_SLABS * LANES - _REC_PERM.size))).astype(bf16)
            p = norm_mod_project(x, g_mix_pre[l], mod_l, w_in)
            gates = gate_activations(p, c_a_log[j], c_dt_bias[j], d_b_i[j], d_b_f[j])
            g_ctx = _gates_rows(gates, 0, BATCH, SEQ)
            g_lat = _gates_rows(gates, N_CTX, DEC_BATCH, DEC_SEQ)
            conv_w = jnp.transpose(c_conv[j].astype(f32).reshape(3, 3 * HEADS, HD), (1, 0, 2))
            zc = jnp.zeros((BATCH, 2, HEADS, HD, HD), f32)
            oc_ctx, sc_ctx = delta_mixer(p, g_ctx, conv_w, c_norm[j].astype(f32), zc, SEQ, 0)
            oc_lat, _ = delta_mixer(p, g_lat, conv_w, c_norm[j].astype(f32), state_c[:, j].astype(f32), DEC_SEQ, N_CTX)
            od_ctx, cd_ctx, nd_ctx, md_ctx = mlstm_mixer(
                p, g_ctx, d_norm[j].astype(f32), zc, jnp.zeros((BATCH, 2, HEADS, HD), f32),
                jnp.zeros((BATCH, 2, HEADS), f32), SEQ, 0)
            od_lat, _, _, _ = mlstm_mixer(
                p, g_lat, d_norm[j].astype(f32), state_d_c[:, j].astype(f32), state_d_n[:, j].astype(f32),
                state_d_m[:, j].astype(f32), DEC_SEQ, N_CTX)
            rec_states.append((sc_ctx, cd_ctx, nd_ctx.reshape(BATCH, 2, HEADS, HD), md_ctx[:, :, :, 0, 0]))
            o1 = jnp.concatenate([oc_ctx, oc_lat], axis=0)
            o2 = jnp.concatenate([od_ctx, od_lat], axis=0)
        x = out_project_residual(o1, o2, w_out[l].astype(bf16), x, g_mix_post[l], mod_l)
        x = mlp_residual(x, g_ffn_pre[l], g_ffn_post[l], mod_l, w_up[l].astype(bf16), w_down[l].astype(bf16))

    y_prompt = x[:N_CTX].reshape(BATCH, SEQ, D_MODEL)
    y_sample = x[N_CTX:].reshape(DEC_BATCH, DEC_SEQ, D_MODEL)

    def kv(slab0, n):
        per = [jnp.transpose(a[slab0:slab0 + n].reshape(n, BATCH, SEQ, HD), (1, 2, 0, 3)) for a in att_kv]
        return jnp.stack(per, axis=1)

    new_a_k, new_a_v = kv(16, A_KV_HEADS), kv(18, A_KV_HEADS)
    new_b_k, new_b_v = kv(0, HEADS), kv(8, HEADS)
    new_state_c = jnp.stack([s[0] for s in rec_states], axis=1)
    new_state_d_c = jnp.stack([s[1] for s in rec_states], axis=1)
    new_state_d_n = jnp.stack([s[2] for s in rec_states], axis=1)
    new_state_d_m = jnp.stack([s[3] for s in rec_states], axis=1)
    return (y_prompt, y_sample, new_a_k, new_a_v, new_b_k, new_b_v, new_state_c, new_state_d_c, new_state_d_n,
            new_state_d_m)
```

```python
import functools
import math

import numpy as np
import jax
import jax.numpy as jnp
from jax import lax
from jax.experimental import pallas as pl
from jax.experimental.pallas import tpu as pltpu

f32 = jnp.float32
bf16 = jnp.bfloat16
HI = lax.Precision.HIGHEST

D_MODEL = 2048
BATCH = 16
SEQ = 256
DEPTH = 4
DEC_BATCH = 8
DEC_SEQ = 2048
PAST_LEN = 512
GRID_W = 64
HD = 128
HEADS = 8
A_KV_HEADS = 2
A_REP = 4
FF = 4 * D_MODEL
WINDOW = 128
NA_ROWS = 8
NA_COLS = 16
CHUNK = 64
ROPE_BASE = 10000.0
EPS = 1e-6
N_ATT = 2
N_REC = 2
ATTN_SCALE = HD ** -0.5
GRID_ROWS = DEC_SEQ // GRID_W

N_CTX = BATCH * SEQ
N_LAT = DEC_BATCH * DEC_SEQ
N_ALL = N_CTX + N_LAT
MOD_ROWS = 16

ATT_SLABS = 36
REC_SLABS = 66
LANES = 128
VMEM_LIMIT = 56 * 1024 * 1024

NT = (((1,), (1,)), ((), ()))
TN_ = (((0,), (0,)), ((), ()))


def _cparams(sem):
    return pltpu.CompilerParams(dimension_semantics=sem, vmem_limit_bytes=VMEM_LIMIT)


def _sigmoid(x):
    return 1.0 / (1.0 + jnp.exp(-x))


def _softplus(x):
    e = jnp.exp(-jnp.abs(x))
    u = 1.0 + e
    log1p = jnp.where(u == 1.0, e, jnp.log(u) * (e / (u - 1.0)))
    return jnp.maximum(x, 0.0) + log1p


def _mod_row(tm):
    ctx_tiles = N_CTX // tm
    per_batch = DEC_SEQ // tm
    return lambda i: jnp.where(i < ctx_tiles, 0, 1 + (i - ctx_tiles) // per_batch)


def _adaln_body(c_ref, w_ref, b_ref, o_ref):
    c = c_ref[...]
    s = (c * _sigmoid(c)).astype(bf16)
    o_ref[0] = jnp.dot(s, w_ref[0].astype(bf16), preferred_element_type=f32) + b_ref[0]


def adaln_all(c_all, w_mod, b_mod):
    tn = 1024
    return pl.pallas_call(
        _adaln_body,
        out_shape=jax.ShapeDtypeStruct((DEPTH, MOD_ROWS, 6 * D_MODEL), f32),
        grid=(DEPTH, 6 * D_MODEL // tn),
        in_specs=[pl.BlockSpec((MOD_ROWS, D_MODEL), lambda l, j: (0, 0)),
                  pl.BlockSpec((1, D_MODEL, tn), lambda l, j: (l, 0, j)),
                  pl.BlockSpec((1, 1, tn), lambda l, j: (l, 0, j))],
        out_specs=pl.BlockSpec((1, MOD_ROWS, tn), lambda l, j: (l, 0, j)),
        compiler_params=_cparams(("arbitrary", "arbitrary")),
        name="adaln",
    )(c_all, w_mod, b_mod.reshape(DEPTH, 1, 6 * D_MODEL))


def _norm_mod_rows(x_ref, g_ref, m_ref, h_ref, tm, sub, sh_row, sc_row):
    g = g_ref[...]
    sc = 1.0 + m_ref[0, sc_row:sc_row + 1, :]
    sh = m_ref[0, sh_row:sh_row + 1, :]

    def body(r, carry):
        rows = pl.ds(pl.multiple_of(r * sub, sub), sub)
        x = x_ref[rows, :]
        ms = jnp.mean(x * x, axis=-1, keepdims=True)
        y = x * lax.rsqrt(ms + EPS) * g
        h_ref[rows, :] = (y * sc + sh).astype(bf16)
        return carry

    lax.fori_loop(0, tm // sub, body, 0)


def _post_norm_residual_rows(y_ref, x_ref, g_ref, m_ref, tm, sub, gate_row):
    g = g_ref[...]
    gt = m_ref[0, gate_row:gate_row + 1, :]

    def body(r, carry):
        rows = pl.ds(pl.multiple_of(r * sub, sub), sub)
        y = y_ref[rows, :]
        ms = jnp.mean(y * y, axis=-1, keepdims=True)
        y_ref[rows, :] = x_ref[rows, :] + gt * (y * lax.rsqrt(ms + EPS) * g)
        return carry

    lax.fori_loop(0, tm // sub, body, 0)


def _proj_body(x_ref, g_ref, m_ref, w_ref, o_ref, h_ref, *, tm, tn, sub):
    @pl.when(pl.program_id(1) == 0)
    def _():
        _norm_mod_rows(x_ref, g_ref, m_ref, h_ref, tm, sub, 0, 1)

    acc = jnp.dot(h_ref[...], w_ref[...], preferred_element_type=f32)
    for s in range(tn // LANES):
        o_ref[s] = acc[:, s * LANES:(s + 1) * LANES]


def norm_mod_project(x, g_pre, mod_l, w_bf16):
    tm, tn, sub = 1024, 768, 64
    ncols = w_bf16.shape[1]
    mrow = _mod_row(tm)
    return pl.pallas_call(
        functools.partial(_proj_body, tm=tm, tn=tn, sub=sub),
        out_shape=jax.ShapeDtypeStruct((ncols // LANES, N_ALL, LANES), f32),
        grid=(N_ALL // tm, ncols // tn),
        in_specs=[pl.BlockSpec((tm, D_MODEL), lambda i, j: (i, 0)),
                  pl.BlockSpec((1, D_MODEL), lambda i, j: (0, 0)),
                  pl.BlockSpec((1, 6, D_MODEL), lambda i, j: (mrow(i), 0, 0)),
                  pl.BlockSpec((D_MODEL, tn), lambda i, j: (0, j))],
        out_specs=pl.BlockSpec((tn // LANES, tm, LANES), lambda i, j: (j, i, 0)),
        scratch_shapes=[pltpu.VMEM((tm, D_MODEL), bf16)],
        compiler_params=_cparams(("arbitrary", "arbitrary")),
        name="norm_mod_project",
    )(x, g_pre.reshape(1, D_MODEL), mod_l, w_bf16)


def _outproj_body(o1_ref, o2_ref, w_ref, x_ref, g_ref, m_ref, out_ref, *, tm, sub):
    half = D_MODEL // 2
    out_ref[...] = (jnp.dot(o1_ref[...], w_ref[0:half, :], preferred_element_type=f32)
                    + jnp.dot(o2_ref[...], w_ref[half:D_MODEL, :], preferred_element_type=f32))
    _post_norm_residual_rows(out_ref, x_ref, g_ref, m_ref, tm, sub, 2)


def out_project_residual(o1, o2, w_bf16, x, g_post, mod_l):
    tm, sub = 512, 64
    half = D_MODEL // 2
    mrow = _mod_row(tm)
    return pl.pallas_call(
        functools.partial(_outproj_body, tm=tm, sub=sub),
        out_shape=jax.ShapeDtypeStruct((N_ALL, D_MODEL), f32),
        grid=(N_ALL // tm,),
        in_specs=[pl.BlockSpec((tm, half), lambda i: (i, 0)),
                  pl.BlockSpec((tm, half), lambda i: (i, 0)),
                  pl.BlockSpec((D_MODEL, D_MODEL), lambda i: (0, 0)),
                  pl.BlockSpec((tm, D_MODEL), lambda i: (i, 0)),
                  pl.BlockSpec((1, D_MODEL), lambda i: (0, 0)),
                  pl.BlockSpec((1, 6, D_MODEL), lambda i: (mrow(i), 0, 0))],
        out_specs=pl.BlockSpec((tm, D_MODEL), lambda i: (i, 0)),
        compiler_params=_cparams(("arbitrary",)),
        name="out_project_residual",
    )(o1, o2, w_bf16, x, g_post.reshape(1, D_MODEL), mod_l)


def _mlp_body(x_ref, gpre_ref, gpost_ref, m_ref, wu_ref, wd_ref, out_ref, h_ref, *, tm, sub, nj):
    j = pl.program_id(1)

    @pl.when(j == 0)
    def _():
        _norm_mod_rows(x_ref, gpre_ref, m_ref, h_ref, tm, sub, 3, 4)

    u = jnp.dot(h_ref[...], wu_ref[...], preferred_element_type=f32)
    u = jnp.maximum(u, 0.0)
    u = (u * u).astype(bf16)
    part = jnp.dot(u, wd_ref[...], preferred_element_type=f32)

    @pl.when(j == 0)
    def _():
        out_ref[...] = part

    @pl.when(j > 0)
    def _():
        out_ref[...] += part

    @pl.when(j == nj - 1)
    def _():
        _post_norm_residual_rows(out_ref, x_ref, gpost_ref, m_ref, tm, sub, 5)


def mlp_residual(x, g_pre, g_post, mod_l, wu_bf16, wd_bf16):
    tm, tf, sub = 512, 1024, 64
    nj = FF // tf
    mrow = _mod_row(tm)
    return pl.pallas_call(
        functools.partial(_mlp_body, tm=tm, sub=sub, nj=nj),
        out_shape=jax.ShapeDtypeStruct((N_ALL, D_MODEL), f32),
        grid=(N_ALL // tm, nj),
        in_specs=[pl.BlockSpec((tm, D_MODEL), lambda i, j: (i, 0)),
                  pl.BlockSpec((1, D_MODEL), lambda i, j: (0, 0)),
                  pl.BlockSpec((1, D_MODEL), lambda i, j: (0, 0)),
                  pl.BlockSpec((1, 6, D_MODEL), lambda i, j: (mrow(i), 0, 0)),
                  pl.BlockSpec((D_MODEL, tf), lambda i, j: (0, j)),
                  pl.BlockSpec((tf, D_MODEL), lambda i, j: (j, 0))],
        out_specs=pl.BlockSpec((tm, D_MODEL), lambda i, j: (i, 0)),
        scratch_shapes=[pltpu.VMEM((tm, D_MODEL), bf16)],
        compiler_params=_cparams(("arbitrary", "arbitrary")),
        name="mlp_residual",
    )(x, g_pre.reshape(1, D_MODEL), g_post.reshape(1, D_MODEL), mod_l, wu_bf16, wd_bf16)


def _rope_tables():
    nf = HD // 4
    t = jnp.arange(DEC_SEQ)
    row = (t // GRID_W).astype(f32)
    col = (t % GRID_W).astype(f32)
    inv = jnp.exp(-math.log(ROPE_BASE) * jnp.arange(nf, dtype=f32) / nf)
    lane = np.arange(HD)
    first_half = (lane % (2 * nf)) < nf
    pos = jnp.where(jnp.asarray(lane < HD // 2)[None, :], row[:, None], col[:, None])
    ang = pos * inv[jnp.asarray(lane % nf)][None, :]
    cos, sin = jnp.cos(ang), jnp.sin(ang)
    fh = jnp.asarray(first_half)[None, :]
    sin_a = jnp.where(fh, -sin, 0.0)
    sin_b = jnp.where(fh, 0.0, sin)

    def tab(a, ident):
        a = a.reshape(2, DEC_SEQ // 2, HD)
        return jnp.concatenate([jnp.full((1, DEC_SEQ // 2, HD), ident, f32), a], axis=0)

    return tab(cos, 1.0), tab(sin_a, 0.0), tab(sin_b, 0.0)


def _prep_att_body(p_ref, cos_ref, sa_ref, sb_ref, o_ref, *, tm):
    s = pl.program_id(0)
    i = pl.program_id(1)
    ctx_tiles = N_CTX // tm
    roped = jnp.logical_and(jnp.logical_or(s < HEADS, s >= 32) & (s < 34), i >= ctx_tiles)
    t = jnp.where(roped, 1 + (i - ctx_tiles) % 2, 0)
    scale = jnp.where(s < 2 * HEADS, ATTN_SCALE, 1.0).astype(f32)
    x = p_ref[0]
    y = x * cos_ref[t] + pltpu.roll(x, 96, 1) * sa_ref[t] + pltpu.roll(x, 32, 1) * sb_ref[t]
    o_ref[0] = (y * scale).astype(bf16)


def prep_att(p_att, tables):
    tm = DEC_SEQ // 2
    cos, sa, sb = tables
    tspec = pl.BlockSpec((3, tm, HD), lambda s, i: (0, 0, 0))
    return pl.pallas_call(
        functools.partial(_prep_att_body, tm=tm),
        out_shape=jax.ShapeDtypeStruct((ATT_SLABS, N_ALL, HD), bf16),
        grid=(ATT_SLABS, N_ALL // tm),
        in_specs=[pl.BlockSpec((1, tm, HD), lambda s, i: (s, i, 0)), tspec, tspec, tspec],
        out_specs=pl.BlockSpec((1, tm, HD), lambda s, i: (s, i, 0)),
        compiler_params=_cparams(("arbitrary", "arbitrary")),
        name="prep_att",
    )(p_att, cos, sa, sb)


def _ctx_attn_body(*refs, reps, use_sink):
    if use_sink:
        sink_ref, q_ref, k_ref, v_ref, o_ref = refs
    else:
        q_ref, k_ref, v_ref, o_ref = refs
    g = pl.program_id(1)
    k = k_ref[0]
    v = v_ref[0]
    for r in range(reps):
        s = lax.dot_general(q_ref[r], k, NT, preferred_element_type=f32)
        m = jnp.max(s, axis=-1, keepdims=True)
        if use_sink:
            sk = sink_ref[g, r]
            m = jnp.maximum(m, sk)
        p = jnp.exp(s - m)
        den = jnp.sum(p, axis=-1, keepdims=True)
        if use_sink:
            den = den + jnp.exp(sk - m)
        o = jnp.dot(p.astype(bf16), v, preferred_element_type=f32) / den
        o_ref[:, r * HD:(r + 1) * HD] = o.astype(bf16)


def ctx_attention_a(pb, sink):
    return pl.pallas_call(
        functools.partial(_ctx_attn_body, reps=A_REP, use_sink=True),
        out_shape=jax.ShapeDtypeStruct((N_CTX, HEADS * HD), bf16),
        grid=(BATCH, A_KV_HEADS),
        in_specs=[pl.BlockSpec(memory_space=pltpu.SMEM),
                  pl.BlockSpec((A_REP, SEQ, HD), lambda b, g: (g, b, 0)),
                  pl.BlockSpec((1, SEQ, HD), lambda b, g: (32 + g, b, 0)),
                  pl.BlockSpec((1, SEQ, HD), lambda b, g: (34 + g, b, 0))],
        out_specs=pl.BlockSpec((SEQ, A_REP * HD), lambda b, g: (b, g)),
        compiler_params=_cparams(("arbitrary", "arbitrary")),
        name="ctx_attention_a",
    )(sink, pb, pb, pb)


def ctx_attention_b(pb):
    return pl.pallas_call(
        functools.partial(_ctx_attn_body, reps=1, use_sink=False),
        out_shape=jax.ShapeDtypeStruct((N_CTX, HEADS * HD), bf16),
        grid=(BATCH, HEADS),
        in_specs=[pl.BlockSpec((1, SEQ, HD), lambda b, h: (8 + h, b, 0)),
                  pl.BlockSpec((1, SEQ, HD), lambda b, h: (16 + h, b, 0)),
                  pl.BlockSpec((1, SEQ, HD), lambda b, h: (24 + h, b, 0))],
        out_specs=pl.BlockSpec((SEQ, HD), lambda b, h: (b, h)),
        compiler_params=_cparams(("arbitrary", "arbitrary")),
        name="ctx_attention_b",
    )(pb, pb, pb)


def _window_body(sink_ref, q_ref, k0_ref, k1_ref, k2_ref, v0_ref, v1_ref, v2_ref, kc_ref, vc_ref, o_ref, *, nb):
    g = pl.program_id(1)
    n = pl.program_id(2)
    rows = A_REP * WINDOW
    q = q_ref[...].reshape(rows, HD)
    kc = kc_ref[0, 0].astype(bf16)
    vc = vc_ref[0, 0].astype(bf16)
    s0 = lax.dot_general(q, k0_ref[0], NT, preferred_element_type=f32)
    s1 = lax.dot_general(q, k1_ref[0], NT, preferred_element_type=f32)
    s2 = lax.dot_general(q, k2_ref[0], NT, preferred_element_type=f32)
    sc = lax.dot_general(q, kc, NT, preferred_element_type=f32)
    qi = lax.broadcasted_iota(jnp.int32, (rows, WINDOW), 0) % WINDOW
    kj = lax.broadcasted_iota(jnp.int32, (rows, WINDOW), 1)
    neg = -jnp.inf
    s0 = jnp.where(kj >= qi, s0, neg)
    s0 = jnp.where(n > 0, s0, neg)
    s2 = jnp.where(kj <= qi, s2, neg)
    s2 = jnp.where(n < nb - 1, s2, neg)
    rep = lax.broadcasted_iota(jnp.int32, (rows, 1), 0) // WINDOW
    sk = jnp.zeros((rows, 1), f32)
    for r in range(A_REP):
        sk = jnp.where(rep == r, sink_ref[g, r], sk)
    m = jnp.maximum(jnp.maximum(jnp.max(s0, axis=-1, keepdims=True), jnp.max(s1, axis=-1, keepdims=True)),
                    jnp.maximum(jnp.max(s2, axis=-1, keepdims=True), jnp.max(sc, axis=-1, keepdims=True)))
    m = jnp.maximum(m, sk)
    p0 = jnp.exp(s0 - m)
    p1 = jnp.exp(s1 - m)
    p2 = jnp.exp(s2 - m)
    pc = jnp.exp(sc - m)
    den = (jnp.sum(p0, axis=-1, keepdims=True) + jnp.sum(p1, axis=-1, keepdims=True)
           + jnp.sum(p2, axis=-1, keepdims=True) + jnp.sum(pc, axis=-1, keepdims=True) + jnp.exp(sk - m))
    o = (jnp.dot(p0.astype(bf16), v0_ref[0], preferred_element_type=f32)
         + jnp.dot(p1.astype(bf16), v1_ref[0], preferred_element_type=f32)
         + jnp.dot(p2.astype(bf16), v2_ref[0], preferred_element_type=f32)
         + jnp.dot(pc.astype(bf16), vc, preferred_element_type=f32)) / den
    for r in range(A_REP):
        o_ref[:, r * HD:(r + 1) * HD] = o[r * WINDOW:(r + 1) * WINDOW].astype(bf16)


def window_attention_lat(pb, sink, ka_c, va_c, j):
    nb = DEC_SEQ // WINDOW
    base = N_CTX // WINDOW

    def qrow(b, n):
        return base + b * nb + n

    def kspec(slab0, shift):
        return pl.BlockSpec((1, WINDOW, HD),
                            lambda b, g, n: (slab0 + g, qrow(b, jnp.clip(n + shift, 0, nb - 1)), 0))

    cspec = pl.BlockSpec((1, 1, PAST_LEN, HD), lambda b, g, n: (b, j, 0, g))
    return pl.pallas_call(
        functools.partial(_window_body, nb=nb),
        out_shape=jax.ShapeDtypeStruct((N_LAT, HEADS * HD), bf16),
        grid=(DEC_BATCH, A_KV_HEADS, nb),
        in_specs=[pl.BlockSpec(memory_space=pltpu.SMEM),
                  pl.BlockSpec((A_REP, WINDOW, HD), lambda b, g, n: (g, qrow(b, n), 0)),
                  kspec(32, -1), kspec(32, 0), kspec(32, 1),
                  kspec(34, -1), kspec(34, 0), kspec(34, 1),
                  cspec, cspec],
        out_specs=pl.BlockSpec((WINDOW, A_REP * HD), lambda b, g, n: (b * nb + n, g)),
        compiler_params=_cparams(("arbitrary", "arbitrary", "arbitrary")),
        name="window_attention",
    )(sink, pb, pb, pb, pb, pb, pb, pb, ka_c, va_c)


NA_QROWS = 2
NA_WROWS = NA_QROWS + NA_ROWS - 1
NA_WIN_MAX = GRID_ROWS - NA_WROWS
NA_OFFS = (0, -2, -4, -5, -7)
NA_Q = NA_QROWS * GRID_W
NA_K = NA_WROWS * GRID_W


def _na_bias_tables(rel_bias):
    r0_of = {0: 0, -2: 2, -4: 4, -5: 28, -7: 30}
    cq = np.arange(GRID_W)[:, None]
    ck = np.arange(GRID_W)[None, :]
    wstart = np.clip(cq - NA_COLS // 2, 0, GRID_W - NA_COLS)
    col_ok = (ck >= wstart) & (ck < wstart + NA_COLS)
    dc = np.clip(ck - cq + NA_COLS - 1, 0, 2 * NA_COLS - 2)
    pick = ((dc[None] == np.arange(2 * NA_COLS - 1)[:, None, None]) & col_ok[None]).astype(np.float32)
    toep = jnp.einsum("hrd,dqk->hrqk", rel_bias, jnp.asarray(pick), precision=HI)
    toep = jnp.where(jnp.asarray(col_ok)[None, None], toep, -1e30)
    masked = jnp.full((HEADS, GRID_W, GRID_W), -1e30, f32)
    tables = []
    for off in NA_OFFS:
        r0 = r0_of[off]
        ws = int(np.clip(r0 - NA_ROWS // 2, 0, NA_WIN_MAX))
        assert ws - r0 == off
        qrows = []
        for a in range(NA_QROWS):
            r = r0 + a
            rs = int(np.clip(r - NA_ROWS // 2, 0, GRID_ROWS - NA_ROWS))
            blocks = []
            for w in range(NA_WROWS):
                krow = ws + w
                blocks.append(toep[:, krow - r + NA_ROWS - 1] if rs <= krow < rs + NA_ROWS else masked)
            qrows.append(jnp.concatenate(blocks, axis=2))
        tables.append(jnp.concatenate(qrows, axis=1))
    return jnp.stack(tables).astype(f32)


def _na_body(bias_ref, q_ref, k_ref, v_ref, kc_ref, vc_ref, o_ref, kcb_ref, vcb_ref):
    rb = pl.program_id(1)

    @pl.when(rb == 0)
    def _():
        kcb_ref[...] = kc_ref[0, 0].astype(bf16)
        vcb_ref[...] = vc_ref[0, 0].astype(bf16)

    ws = jnp.clip(NA_QROWS * rb - NA_ROWS // 2, 0, NA_WIN_MAX)
    win = pl.ds(pl.multiple_of(ws * GRID_W, GRID_W), NA_K)
    for h in range(HEADS):
        q = q_ref[h]
        lanes = slice(h * HD, (h + 1) * HD)
        sl = lax.dot_general(q, k_ref[h, win, :], NT, preferred_element_type=f32) + bias_ref[0, h]
        sc = lax.dot_general(q, kcb_ref[:, lanes], NT, preferred_element_type=f32)
        m = jnp.maximum(jnp.max(sl, axis=-1, keepdims=True), jnp.max(sc, axis=-1, keepdims=True))
        p_l = jnp.exp(sl - m)
        p_c = jnp.exp(sc - m)
        den = jnp.sum(p_l, axis=-1, keepdims=True) + jnp.sum(p_c, axis=-1, keepdims=True)
        o = (jnp.dot(p_l.astype(bf16), v_ref[h, win, :], preferred_element_type=f32)
             + jnp.dot(p_c.astype(bf16), vcb_ref[:, lanes], preferred_element_type=f32)) / den
        o_ref[:, lanes] = o.astype(bf16)


def neighbourhood_attention_lat(pb, bias_tab, kb_c, vb_c, j):
    nrb = GRID_ROWS // NA_QROWS
    base = N_CTX // NA_Q
    per_b = DEC_SEQ // NA_Q
    seq_base = N_CTX // DEC_SEQ

    def tbl(rb):
        return jnp.where(rb < 2, rb, jnp.where(rb < nrb - 2, 2, rb - (nrb - 5)))

    cspec = pl.BlockSpec((1, 1, PAST_LEN, HEADS * HD), lambda b, rb: (b, j, 0, 0))
    return pl.pallas_call(
        _na_body,
        out_shape=jax.ShapeDtypeStruct((N_LAT, HEADS * HD), bf16),
        grid=(DEC_BATCH, nrb),
        in_specs=[pl.BlockSpec((1, HEADS, NA_Q, NA_K), lambda b, rb: (tbl(rb), 0, 0, 0)),
                  pl.BlockSpec((HEADS, NA_Q, HD), lambda b, rb: (1, base + b * per_b + rb, 0)),
                  pl.BlockSpec((HEADS, DEC_SEQ, HD), lambda b, rb: (2, seq_base + b, 0)),
                  pl.BlockSpec((HEADS, DEC_SEQ, HD), lambda b, rb: (3, seq_base + b, 0)),
                  cspec, cspec],
        out_specs=pl.BlockSpec((NA_Q, HEADS * HD), lambda b, rb: (b * per_b + rb, 0)),
        scratch_shapes=[pltpu.VMEM((PAST_LEN, HEADS * HD), bf16), pltpu.VMEM((PAST_LEN, HEADS * HD), bf16)],
        compiler_params=_cparams(("arbitrary", "arbitrary")),
        name="neighbourhood_attention",
    )(bias_tab, pb, pb, pb, kb_c, vb_c)


def _gates_body(p_ref, par_ref, o_ref):
    x = p_ref[0]
    lane = lax.broadcasted_iota(jnp.int32, x.shape, 1)
    z = x + par_ref[1:2, :]
    sig = _sigmoid(z)
    g = par_ref[0:1, :] * _softplus(z)
    lsig = -_softplus(-z)
    o_ref[...] = jnp.where(lane < 16, sig, jnp.where(lane < 32, g, jnp.where(lane < 48, z, lsig)))


def gate_activations(p_rec, a_log, dt_bias, b_i, b_f):
    tm = 1024
    zeros16 = jnp.zeros((16,), f32)
    neg_a = jnp.concatenate([zeros16, -jnp.exp(a_log.reshape(16).astype(f32)), zeros16, zeros16, jnp.zeros((64,), f32)])
    bias = jnp.concatenate([zeros16, dt_bias.reshape(16).astype(f32), b_i.reshape(16).astype(f32),
                            b_f.reshape(16).astype(f32), jnp.zeros((64,), f32)])
    par = jnp.concatenate([neg_a[None], bias[None], jnp.zeros((6, LANES), f32)], axis=0)
    return pl.pallas_call(
        _gates_body,
        out_shape=jax.ShapeDtypeStruct((N_ALL, LANES), f32),
        grid=(N_ALL // tm,),
        in_specs=[pl.BlockSpec((1, tm, LANES), lambda i: (64, i, 0)),
                  pl.BlockSpec((8, LANES), lambda i: (0, 0))],
        out_specs=pl.BlockSpec((tm, LANES), lambda i: (i, 0)),
        compiler_params=_cparams(("arbitrary",)),
        name="gate_activations",
    )(p_rec, par)


def _tri(reverse, strict):
    ii = lax.broadcasted_iota(jnp.int32, (CHUNK, CHUNK), 0)
    jj = lax.broadcasted_iota(jnp.int32, (CHUNK, CHUNK), 1)
    if reverse:
        return (jj > ii) if strict else (jj >= ii)
    return (jj < ii) if strict else (jj <= ii)


def _cumsum_rows(x, reverse):
    aa = lax.broadcasted_iota(jnp.int32, (CHUNK, CHUNK), 0)
    bb = lax.broadcasted_iota(jnp.int32, (CHUNK, CHUNK), 1)
    u = jnp.where((aa >= bb) if reverse else (aa <= bb), 1.0, 0.0).astype(f32)
    return jnp.dot(x, u, precision=HI, preferred_element_type=f32)


def _split(x):
    hi = x.astype(bf16)
    return hi, (x - hi.astype(f32)).astype(bf16)


def _mm3(a, b):
    ah, al = _split(a)
    bh, bl = _split(b)
    return (jnp.dot(ah, bh, preferred_element_type=f32) + jnp.dot(ah, bl, preferred_element_type=f32)
            + jnp.dot(al, bh, preferred_element_type=f32))


def _unit_tri_inverses(mats):
    ii = lax.broadcasted_iota(jnp.int32, (CHUNK, CHUNK), 0)
    jj = lax.broadcasted_iota(jnp.int32, (CHUNK, CHUNK), 1)
    same16 = (ii // 16) == (jj // 16)
    same32 = (ii // 32) == (jj // 32)
    eye = jnp.where(ii == jj, 1.0, 0.0).astype(f32)
    m = [jnp.where(same16, -a, 0.0) for a in mats]
    l1 = [jnp.where(same16, 0.0, jnp.where(same32, a, 0.0)) for a in mats]
    l2 = [jnp.where(same32, 0.0, a) for a in mats]
    p = [eye + x for x in m]
    m = [_mm3(x, x) for x in m]
    for _ in range(2):
        r = [_mm3(jnp.concatenate([x, y], axis=0), x) for x, y in zip(m, p)]
        m = [x[:CHUNK] for x in r]
        p = [y + x[CHUNK:] for x, y in zip(r, p)]
    p = [y + _mm3(y, x) for x, y in zip(m, p)]
    p = [y - _mm3(y, _mm3(l, y)) for l, y in zip(l1, p)]
    return [y - _mm3(y, _mm3(l, y)) for l, y in zip(l2, p)]


def _rows_to_cols(r0, r1):
    ri = lax.broadcasted_iota(jnp.int32, (8, CHUNK), 0)
    x = jnp.where(ri == 0, jnp.broadcast_to(r0, (8, CHUNK)), jnp.where(ri == 1, jnp.broadcast_to(r1, (8, CHUNK)), 0.0))
    eye = jnp.where(_tri(False, False) & _tri(True, False), 1.0, 0.0).astype(f32)
    cols = lax.dot_general(eye, x, NT, precision=HI, preferred_element_type=f32)
    return cols[:, 0:1], cols[:, 1:2]


PREP_CHUNKS = 4


def _chunk_rows(c):
    return pl.ds(pl.multiple_of(c * CHUNK, CHUNK), CHUNK)


def _conv_silu(x_ref, cw_ref, T):
    x = x_ref[0]
    w = cw_ref[0]
    row = lax.broadcasted_iota(jnp.int32, (T, HD), 0)
    xp = jnp.where(row == 0, 0.0, pltpu.roll(x, 1, 0))
    xn = jnp.where(row == T - 1, 0.0, pltpu.roll(x, T - 1, 0))
    y = w[0:1] * xp + w[1:2] * x + w[2:3] * xn
    return y * _sigmoid(y)


def _delta_body(xq_ref, xk_ref, xv_ref, xz_ref, cwq_ref, cwk_ref, cwv_ref, bf_ref, bb_ref, gf_ref, gb_ref,
                s0_ref, cn_ref, o_ref, sout_ref,
                q_s, k_s, v_s, gc_s, u_s, wq_s, kd_s, at_s, eg_s, o_s, *, T):
    nc = T // CHUNK
    q = _conv_silu(xq_ref, cwq_ref, T)
    q_s[...] = q * lax.rsqrt(jnp.sum(q * q, axis=-1, keepdims=True) + EPS) * ATTN_SCALE
    k = _conv_silu(xk_ref, cwk_ref, T)
    k_s[...] = k * lax.rsqrt(jnp.sum(k * k, axis=-1, keepdims=True) + EPS)
    v_s[...] = _conv_silu(xv_ref, cwv_ref, T)

    gc_s[0] = _cumsum_rows(gf_ref[0, 0], False)
    gc_s[1] = _cumsum_rows(gb_ref[0, 0], True)
    beta_refs = (bf_ref, bb_ref)
    incl = (_tri(False, False), _tri(True, False))
    strict = (_tri(False, True), _tri(True, True))

    def prep_group(t, carry):
        chains = [(t * PREP_CHUNKS + i, d) for i in range(PREP_CHUNKS) for d in (0, 1)]
        ld = []
        for c, d in chains:
            rows = _chunk_rows(c)
            gcum_r = gc_s[d, pl.ds(c, 1), :]
            gcum_c, beta_c = _rows_to_cols(gcum_r, beta_refs[d][0, 0, pl.ds(c, 1), :])
            ld.append((q_s[rows, :], k_s[rows, :], v_s[rows, :], gcum_r, gcum_c, beta_c))
        decay = [jnp.exp(jnp.where(incl[d], x[4] - x[3], -jnp.inf)) for (c, d), x in zip(chains, ld)]
        kb = [x[1] * x[5] for x in ld]
        a = [jnp.where(strict[d], lax.dot_general(y.astype(bf16), x[1].astype(bf16), NT,
                                                  preferred_element_type=f32) * dc, 0.0)
             for (c, d), x, y, dc in zip(chains, ld, kb, decay)]
        inv = _unit_tri_inverses(a)
        for (c, d), (q, k, v, gcum_r, gcum_c, beta_c), kbi, dc, t_inv in zip(chains, ld, kb, decay, inv):
            rows = _chunk_rows(c)
            eg_c = jnp.exp(gcum_c)
            uw = _mm3(t_inv, jnp.concatenate([v * beta_c, kbi * eg_c], axis=1))
            u_s[d, rows, :] = uw[:, :HD]
            pair = pl.ds(pl.multiple_of(c * 2 * CHUNK, 2 * CHUNK), 2 * CHUNK)
            wq_s[d, pair, :] = jnp.concatenate([uw[:, HD:], q * eg_c], axis=0).astype(bf16)
            att = lax.dot_general(q.astype(bf16), k.astype(bf16), NT, preferred_element_type=f32) * dc
            at_s[d, rows, :] = att.astype(bf16)
            glast = gcum_r[:, 0:1] if d == 1 else gcum_r[:, CHUNK - 1:CHUNK]
            kd_s[d, pair, :] = (k * jnp.exp(glast - gcum_c)).T.astype(bf16)
            eg_s[d, pl.ds(pl.multiple_of(c * 8, 8), 8), :] = jnp.broadcast_to(jnp.exp(glast), (8, HD))
        return carry

    lax.fori_loop(0, nc // PREP_CHUNKS, prep_group, 0)

    def rec(c, d, s):
        rows = _chunk_rows(c)
        pair = pl.ds(pl.multiple_of(c * 2 * CHUNK, 2 * CHUNK), 2 * CHUNK)
        r = jnp.dot(wq_s[d, pair, :], s.astype(bf16), preferred_element_type=f32)
        vn = (u_s[d, rows, :] - r[:CHUNK]).astype(bf16)
        o_s[d, rows, :] = r[CHUNK:] + jnp.dot(at_s[d, rows, :], vn, preferred_element_type=f32)
        eg = eg_s[d, pl.ds(pl.multiple_of(c * 8, 8), 8), :][0:1]
        return s * eg + jnp.dot(kd_s[d, pair, :], vn, preferred_element_type=f32)

    def rec_step(i, carry):
        sf, sb = carry
        return rec(i, 0, sf), rec(nc - 1 - i, 1, sb)

    sf, sb = lax.fori_loop(0, nc, rec_step, (s0_ref[0, 0, 0], s0_ref[0, 1, 0]))
    sout_ref[0, 0, 0] = sf
    sout_ref[0, 1, 0] = sb

    o = o_s[0] + o_s[1]
    o = o * lax.rsqrt(jnp.mean(o * o, axis=-1, keepdims=True) + EPS) * cn_ref[...]
    z = xz_ref[0]
    o_ref[...] = (o * (z * _sigmoid(z))).astype(bf16)


def delta_mixer(p_rec, gates_t, conv_w, c_norm, s0, T, row0):
    B = s0.shape[0]
    nc = T // CHUNK
    rb0 = row0 // T

    def xspec(slab0):
        return pl.BlockSpec((1, T, HD), lambda b, h: (slab0 + h, rb0 + b, 0))

    def cwspec(slab0):
        return pl.BlockSpec((1, 3, HD), lambda b, h: (slab0 + h, 0, 0))

    def gspec(col0):
        return pl.BlockSpec((1, 1, nc, CHUNK), lambda b, h: (b, col0 + h, 0, 0))

    sspec = pl.BlockSpec((1, 2, 1, HD, HD), lambda b, h: (b, 0, h, 0, 0))
    big = pltpu.VMEM((2, T, HD), f32)
    o, sout = pl.pallas_call(
        functools.partial(_delta_body, T=T),
        out_shape=(jax.ShapeDtypeStruct((B * T, HEADS * HD), bf16),
                   jax.ShapeDtypeStruct((B, 2, HEADS, HD, HD), f32)),
        grid=(B, HEADS),
        in_specs=[xspec(0), xspec(8), xspec(16), xspec(24), cwspec(0), cwspec(8), cwspec(16),
                  gspec(0), gspec(8), gspec(16), gspec(24), sspec,
                  pl.BlockSpec((1, HD), lambda b, h: (0, 0))],
        out_specs=(pl.BlockSpec((T, HD), lambda b, h: (b, h)), sspec),
        scratch_shapes=[pltpu.VMEM((T, HD), f32), pltpu.VMEM((T, HD), f32), pltpu.VMEM((T, HD), f32),
                        pltpu.VMEM((2, nc, CHUNK), f32), big, pltpu.VMEM((2, 2 * T, HD), bf16),
                        pltpu.VMEM((2, 2 * T, CHUNK), bf16), pltpu.VMEM((2, T, CHUNK), bf16),
                        pltpu.VMEM((2, nc * 8, HD), f32), big],
        compiler_params=_cparams(("arbitrary", "arbitrary")),
        name="delta_mixer",
    )(p_rec, p_rec, p_rec, p_rec, conv_w, conv_w, conv_w, gates_t, gates_t, gates_t, gates_t, s0,
      c_norm.reshape(1, HD))
    return o, sout


def _mlstm_body(m0_ref, xq_ref, xk_ref, xv_ref, xo_ref, if_ref, ib_ref, ff_ref, fb_ref, c0_ref, n0_ref, dn_ref,
                o_ref, cout_ref, nout_ref, mout_ref,
                bc_s, qk_s, kw_s, col_s, ds_s, h_s, *, T):
    nc = T // CHUNK
    b_id = pl.program_id(0)
    h_id = pl.program_id(1)
    bc_s[0] = _cumsum_rows(ff_ref[0, 0], False)
    bc_s[1] = _cumsum_rows(fb_ref[0, 0], True)
    i_refs = (if_ref, ib_ref)
    incl = (_tri(False, False), _tri(True, False))
    lane = lax.broadcasted_iota(jnp.int32, (CHUNK, HD), 1)
    row8 = lax.broadcasted_iota(jnp.int32, (8, HD), 0)

    def prep_group(t, carry):
        chains = [(t * PREP_CHUNKS + i if d == 0 else nc - 1 - (t * PREP_CHUNKS + i), d)
                  for i in range(PREP_CHUNKS) for d in (0, 1)]
        ld = []
        for c, d in chains:
            rows = _chunk_rows(c)
            q = (xq_ref[0, rows, :] * ATTN_SCALE).astype(bf16)
            k = xk_ref[0, rows, :]
            i_r = i_refs[d][0, 0, pl.ds(c, 1), :]
            b_r = bc_s[d, pl.ds(c, 1), :]
            b_c, i_c = _rows_to_cols(b_r, i_r)
            dmat = jnp.where(incl[d], b_c - b_r + i_r, -jnp.inf)
            dmax = jnp.max(dmat, axis=-1, keepdims=True)
            qk = lax.dot_general(q, k.astype(bf16), NT, preferred_element_type=f32)
            ld.append((k, b_r, b_c, i_c, dmat, dmax, qk))
        m = list(carry)
        for (c, d), (k, b_r, b_c, i_c, dmat, dmax, qk) in zip(chains, ld):
            last = 0 if d == 1 else CHUNK - 1
            rows = _chunk_rows(c)
            inter = b_c + m[d]
            mt = jnp.maximum(inter, dmax)
            qkw = qk * jnp.exp(dmat - mt)
            m_new = mt[last:last + 1, :]
            b_last = b_r[:, last:last + 1]
            kw = k * jnp.exp(b_last - b_c + i_c - m_new)
            qk_s[d, rows, :] = qkw.astype(bf16)
            kw_s[d, pl.ds(pl.multiple_of(c * 2 * CHUNK, 2 * CHUNK), 2 * CHUNK), :] = kw.T.astype(bf16)
            col_s[d, rows, :] = jnp.where(lane == 0, jnp.exp(inter - mt),
                                          jnp.where(lane == 1, jnp.sum(qkw, axis=-1, keepdims=True), jnp.exp(-mt)))
            ds = jnp.broadcast_to(jnp.exp(b_last + m[d] - m_new), (8, HD))
            ksum = jnp.broadcast_to(jnp.sum(kw, axis=0, keepdims=True), (8, HD))
            ds_s[d, pl.ds(pl.multiple_of(c * 8, 8), 8), :] = jnp.where(row8 == 0, ds, ksum)
            m[d] = m_new
        return tuple(m)

    m_init = (jnp.full((1, 1), m0_ref[b_id, 0, h_id], f32), jnp.full((1, 1), m0_ref[b_id, 1, h_id], f32))
    mf, mb = lax.fori_loop(0, nc // PREP_CHUNKS, prep_group, m_init)

    def rec(c, d, cm, n):
        rows = _chunk_rows(c)
        q = xq_ref[0, rows, :] * ATTN_SCALE
        v = xv_ref[0, rows, :].astype(bf16)
        cols = col_s[d, rows, :]
        w_inter = cols[:, 0:1]
        num = (w_inter * jnp.dot(q.astype(bf16), cm.astype(bf16), preferred_element_type=f32)
               + jnp.dot(qk_s[d, rows, :], v, preferred_element_type=f32))
        den = w_inter * jnp.sum(q * n, axis=-1, keepdims=True) + cols[:, 1:2]
        h_s[d, rows, :] = num / jnp.maximum(jnp.abs(den), cols[:, 2:3])
        dsn = ds_s[d, pl.ds(pl.multiple_of(c * 8, 8), 8), :]
        ds = dsn[0:1]
        kwt = kw_s[d, pl.ds(pl.multiple_of(c * 2 * CHUNK, 2 * CHUNK), 2 * CHUNK), :]
        cm = ds * cm + jnp.dot(kwt, v, preferred_element_type=f32)
        n = ds * n + dsn[1:2]
        return cm, n

    def rec_step(i, carry):
        cf, nf, cb, nb = carry
        cf, nf = rec(i, 0, cf, nf)
        cb, nb = rec(nc - 1 - i, 1, cb, nb)
        return cf, nf, cb, nb

    cf, nf, cb, nb = lax.fori_loop(0, nc, rec_step,
                                   (c0_ref[0, 0, 0], n0_ref[0, 0, 0], c0_ref[0, 1, 0], n0_ref[0, 1, 0]))
    cout_ref[0, 0, 0] = cf
    cout_ref[0, 1, 0] = cb
    nout_ref[0, 0, 0] = nf
    nout_ref[0, 1, 0] = nb
    mout_ref[0, 0, 0] = jnp.broadcast_to(mf, (1, HD))
    mout_ref[0, 1, 0] = jnp.broadcast_to(mb, (1, HD))

    h = h_s[0] + h_s[1]
    h = h * lax.rsqrt(jnp.mean(h * h, axis=-1, keepdims=True) + EPS) * dn_ref[...]
    o_ref[...] = (h * _sigmoid(xo_ref[0])).astype(bf16)


def mlstm_mixer(p_rec, gates_t, d_norm, c0, n0, m0, T, row0):
    B = c0.shape[0]
    nc = T // CHUNK
    rb0 = row0 // T

    def xspec(slab0):
        return pl.BlockSpec((1, T, HD), lambda b, h: (slab0 + h, rb0 + b, 0))

    def gspec(col0):
        return pl.BlockSpec((1, 1, nc, CHUNK), lambda b, h: (b, col0 + h, 0, 0))

    cspec = pl.BlockSpec((1, 2, 1, HD, HD), lambda b, h: (b, 0, h, 0, 0))
    vspec = pl.BlockSpec((1, 2, 1, 1, HD), lambda b, h: (b, 0, h, 0, 0))
    big = pltpu.VMEM((2, T, HD), f32)
    o, cout, nout, mout = pl.pallas_call(
        functools.partial(_mlstm_body, T=T),
        out_shape=(jax.ShapeDtypeStruct((B * T, HEADS * HD), bf16),
                   jax.ShapeDtypeStruct((B, 2, HEADS, HD, HD), f32),
                   jax.ShapeDtypeStruct((B, 2, HEADS, 1, HD), f32),
                   jax.ShapeDtypeStruct((B, 2, HEADS, 1, HD), f32)),
        grid=(B, HEADS),
        in_specs=[pl.BlockSpec(memory_space=pltpu.SMEM),
                  xspec(32), xspec(40), xspec(48), xspec(56),
                  gspec(32), gspec(40), gspec(48), gspec(56), cspec, vspec,
                  pl.BlockSpec((1, HD), lambda b, h: (0, 0))],
        out_specs=(pl.BlockSpec((T, HD), lambda b, h: (b, h)), cspec, vspec, vspec),
        scratch_shapes=[pltpu.VMEM((2, nc, CHUNK), f32), pltpu.VMEM((2, T, CHUNK), bf16),
                        pltpu.VMEM((2, 2 * T, CHUNK), bf16), big, pltpu.VMEM((2, nc * 8, HD), f32), big],
        compiler_params=_cparams(("arbitrary", "arbitrary")),
        name="mlstm_mixer",
    )(m0, p_rec, p_rec, p_rec, p_rec, gates_t, gates_t, gates_t, gates_t, c0, n0.reshape(B, 2, HEADS, 1, HD),
      d_norm.reshape(1, HD))
    return o, cout, nout, mout


def _att_weight(w):
    return jnp.concatenate([w[:, :1024], w[:, 1536:], w[:, 1024:1536]], axis=1).astype(bf16)


def _rec_weight(w):
    pad = jnp.zeros((D_MODEL, REC_SLABS * LANES - 8256), w.dtype)
    return jnp.concatenate([w[:, :4096], w[:, 4128:8224], w[:, 4096:4128], w[:, 8224:], pad], axis=1).astype(bf16)


def _gates_rows(gates, row0, B, T):
    g = gates[row0:row0 + B * T, :64].reshape(B, T // CHUNK, CHUNK, 64)
    return jnp.transpose(g, (0, 3, 1, 2))


def kernel(x_prompt, x_sample, cache_a_k, cache_a_v, cache_b_k, cache_b_v, state_c, state_d_c, state_d_n, state_d_m, c, c_ctx, w_mod, b_mod, g_mix_pre, g_mix_post, g_ffn_pre, g_ffn_post, w_in_att, w_in_rec, w_out, a_sink, b_rel_bias, c_conv, c_a_log, c_dt_bias, c_norm, d_b_i, d_b_f, d_norm, w_up, w_down):
    x = jnp.concatenate([x_prompt.reshape(N_CTX, D_MODEL), x_sample.reshape(N_LAT, D_MODEL)], axis=0)
    c_all = jnp.concatenate([c_ctx[None, :], c, jnp.zeros((MOD_ROWS - 1 - DEC_BATCH, D_MODEL), f32)], axis=0)
    mods = adaln_all(c_all, w_mod, b_mod).reshape(DEPTH, MOD_ROWS, 6, D_MODEL)
    tables = _rope_tables()

    ka_c = cache_a_k.reshape(DEC_BATCH, N_ATT, PAST_LEN, A_KV_HEADS * HD)
    va_c = cache_a_v.reshape(DEC_BATCH, N_ATT, PAST_LEN, A_KV_HEADS * HD)
    kb_c = cache_b_k.reshape(DEC_BATCH, N_ATT, PAST_LEN, HEADS * HD)
    vb_c = cache_b_v.reshape(DEC_BATCH, N_ATT, PAST_LEN, HEADS * HD)

    att_kv = []
    rec_states = []
    for l in range(DEPTH):
        j = l // 2
        mod_l = mods[l]
        if l % 2 == 0:
            p = norm_mod_project(x, g_mix_pre[l], mod_l, _att_weight(w_in_att[j]))
            att_kv.append(p[16:36, :N_CTX])
            pb = prep_att(p, tables)
            sink = a_sink[j].astype(f32)
            o1 = jnp.concatenate([ctx_attention_a(pb, sink), window_attention_lat(pb, sink, ka_c, va_c, j)], axis=0)
            bias_tab = _na_bias_tables(b_rel_bias[j].astype(f32))
            o2 = jnp.concatenate([ctx_attention_b(pb), neighbourhood_attention_lat(pb, bias_tab, kb_c, vb_c, j)], axis=0)
        else:
            p = norm_mod_project(x, g_mix_pre[l], mod_l, _rec_weight(w_in_rec[j]))
            gates = gate_activations(p, c_a_log[j], c_dt_bias[j], d_b_i[j], d_b_f[j])
            g_ctx = _gates_rows(gates, 0, BATCH, SEQ)
            g_lat = _gates_rows(gates, N_CTX, DEC_BATCH, DEC_SEQ)
            conv_w = jnp.transpose(c_conv[j].astype(f32).reshape(3, 3 * HEADS, HD), (1, 0, 2))
            zc = jnp.zeros((BATCH, 2, HEADS, HD, HD), f32)
            oc_ctx, sc_ctx = delta_mixer(p, g_ctx, conv_w, c_norm[j].astype(f32), zc, SEQ, 0)
            oc_lat, _ = delta_mixer(p, g_lat, conv_w, c_norm[j].astype(f32), state_c[:, j].astype(f32), DEC_SEQ, N_CTX)
            od_ctx, cd_ctx, nd_ctx, md_ctx = mlstm_mixer(
                p, g_ctx, d_norm[j].astype(f32), zc, jnp.zeros((BATCH, 2, HEADS, HD), f32),
                jnp.zeros((BATCH, 2, HEADS), f32), SEQ, 0)
            od_lat, _, _, _ = mlstm_mixer(
                p, g_lat, d_norm[j].astype(f32), state_d_c[:, j].astype(f32), state_d_n[:, j].astype(f32),
                state_d_m[:, j].astype(f32), DEC_SEQ, N_CTX)
            rec_states.append((sc_ctx, cd_ctx, nd_ctx.reshape(BATCH, 2, HEADS, HD), md_ctx[:, :, :, 0, 0]))
            o1 = jnp.concatenate([oc_ctx, oc_lat], axis=0)
            o2 = jnp.concatenate([od_ctx, od_lat], axis=0)
        x = out_project_residual(o1, o2, w_out[l].astype(bf16), x, g_mix_post[l], mod_l)
        x = mlp_residual(x, g_ffn_pre[l], g_ffn_post[l], mod_l, w_up[l].astype(bf16), w_down[l].astype(bf16))

    y_prompt = x[:N_CTX].reshape(BATCH, SEQ, D_MODEL)
    y_sample = x[N_CTX:].reshape(DEC_BATCH, DEC_SEQ, D_MODEL)

    def kv(slab0, n):
        per = [jnp.transpose(a[slab0:slab0 + n].reshape(n, BATCH, SEQ, HD), (1, 2, 0, 3)) for a in att_kv]
        return jnp.stack(per, axis=1)

    new_a_k, new_a_v = kv(16, A_KV_HEADS), kv(18, A_KV_HEADS)
    new_b_k, new_b_v = kv(0, HEADS), kv(8, HEADS)
    new_state_c = jnp.stack([s[0] for s in rec_states], axis=1)
    new_state_d_c = jnp.stack([s[1] for s in rec_states], axis=1)
    new_state_d_n = jnp.stack([s[2] for s in rec_states], axis=1)
    new_state_d_m = jnp.stack([s[3] for s in rec_states], axis=1)
    return (y_prompt, y_sample, new_a_k, new_a_v, new_b_k, new_b_v, new_state_c, new_state_d_c, new_state_d_n,
            new_state_d_m)
```

```python
import functools
import math

import numpy as np
import jax
import jax.numpy as jnp
from jax import lax
from jax.experimental import pallas as pl
from jax.experimental.pallas import tpu as pltpu

f32 = jnp.float32
bf16 = jnp.bfloat16
HI = lax.Precision.HIGHEST

D_MODEL = 2048
BATCH = 16
SEQ = 256
DEPTH = 4
DEC_BATCH = 8
DEC_SEQ = 2048
PAST_LEN = 512
GRID_W = 64
HD = 128
HEADS = 8
A_KV_HEADS = 2
A_REP = 4
FF = 4 * D_MODEL
WINDOW = 128
NA_ROWS = 8
NA_COLS = 16
CHUNK = 64
ROPE_BASE = 10000.0
EPS = 1e-6
N_ATT = 2
N_REC = 2
ATTN_SCALE = HD ** -0.5
GRID_ROWS = DEC_SEQ // GRID_W

N_CTX = BATCH * SEQ
N_LAT = DEC_BATCH * DEC_SEQ
N_ALL = N_CTX + N_LAT
MOD_ROWS = 16

ATT_SLABS = 36
REC_SLABS = 66
LANES = 128
NORM_ROWS = 64
VMEM_LIMIT = 56 * 1024 * 1024

NT = (((1,), (1,)), ((), ()))
TN_ = (((0,), (0,)), ((), ()))


def _cparams(sem):
    return pltpu.CompilerParams(dimension_semantics=sem, vmem_limit_bytes=VMEM_LIMIT)


def _sigmoid(x):
    return 1.0 / (1.0 + jnp.exp(-x))


def _softplus(x):
    e = jnp.exp(-jnp.abs(x))
    u = 1.0 + e
    log1p = jnp.where(u == 1.0, e, jnp.log(u) * (e / (u - 1.0)))
    return jnp.maximum(x, 0.0) + log1p


def _mod_row(tm):
    ctx_tiles = N_CTX // tm
    per_batch = DEC_SEQ // tm
    return lambda i: jnp.where(i < ctx_tiles, 0, 1 + (i - ctx_tiles) // per_batch)


def _adaln_body(c_ref, w_ref, b_ref, o_ref):
    c = c_ref[...]
    s = (c * _sigmoid(c)).astype(bf16)
    o_ref[0] = jnp.dot(s, w_ref[0].astype(bf16), preferred_element_type=f32) + b_ref[0]


def adaln_all(c_all, w_mod, b_mod):
    tn = 1024
    return pl.pallas_call(
        _adaln_body,
        out_shape=jax.ShapeDtypeStruct((DEPTH, MOD_ROWS, 6 * D_MODEL), f32),
        grid=(DEPTH, 6 * D_MODEL // tn),
        in_specs=[pl.BlockSpec((MOD_ROWS, D_MODEL), lambda l, j: (0, 0)),
                  pl.BlockSpec((1, D_MODEL, tn), lambda l, j: (l, 0, j)),
                  pl.BlockSpec((1, 1, tn), lambda l, j: (l, 0, j))],
        out_specs=pl.BlockSpec((1, MOD_ROWS, tn), lambda l, j: (l, 0, j)),
        compiler_params=_cparams(("arbitrary", "arbitrary")),
        name="adaln",
    )(c_all, w_mod, b_mod.reshape(DEPTH, 1, 6 * D_MODEL))


def _norm_mod_rows(x_ref, g_ref, m_ref, h_ref, tm, sub, sh_row, sc_row):
    gain = g_ref[...] * (1.0 + m_ref[0, sc_row:sc_row + 1, :])
    sh = m_ref[0, sh_row:sh_row + 1, :]

    def body(r, carry):
        rows = pl.ds(pl.multiple_of(r * sub, sub), sub)
        x = x_ref[rows, :]
        ms = jnp.mean(x * x, axis=-1, keepdims=True)
        h_ref[rows, :] = (x * lax.rsqrt(ms + EPS) * gain + sh).astype(bf16)
        return carry

    lax.fori_loop(0, tm // sub, body, 0, unroll=2)


def _post_norm_residual_rows(y_ref, x_ref, g_ref, m_ref, tm, sub, gate_row):
    gain = g_ref[...] * m_ref[0, gate_row:gate_row + 1, :]

    def body(r, carry):
        rows = pl.ds(pl.multiple_of(r * sub, sub), sub)
        y = y_ref[rows, :]
        ms = jnp.mean(y * y, axis=-1, keepdims=True)
        y_ref[rows, :] = x_ref[rows, :] + y * lax.rsqrt(ms + EPS) * gain
        return carry

    lax.fori_loop(0, tm // sub, body, 0, unroll=2)


def _proj_body(x_ref, g_ref, m_ref, w_ref, o_ref, *rest, tm, tn, sub, q_slabs):
    h_ref = rest[-1]

    @pl.when(pl.program_id(1) == 0)
    def _():
        _norm_mod_rows(x_ref, g_ref, m_ref, h_ref, tm, sub, 0, 1)

    acc = jnp.dot(h_ref[...], w_ref[...], preferred_element_type=f32)
    for s in range(tn // LANES):
        o_ref[s] = acc[:, s * LANES:(s + 1) * LANES]
    if q_slabs:
        ob_ref = rest[0]
        slab0 = pl.program_id(1) * (tn // LANES)
        for s in range(tn // LANES):
            scale = jnp.where(slab0 + s < q_slabs, ATTN_SCALE, 1.0).astype(f32)
            ob_ref[s] = (acc[:, s * LANES:(s + 1) * LANES] * scale).astype(bf16)


def norm_mod_project(x, g_pre, mod_l, w_bf16, q_slabs=0):
    tm, tn, sub = 1024, 768, NORM_ROWS
    ncols = w_bf16.shape[1]
    mrow = _mod_row(tm)
    ospec = pl.BlockSpec((tn // LANES, tm, LANES), lambda i, j: (j, i, 0))
    oshape = jax.ShapeDtypeStruct((ncols // LANES, N_ALL, LANES), f32)
    return pl.pallas_call(
        functools.partial(_proj_body, tm=tm, tn=tn, sub=sub, q_slabs=q_slabs),
        out_shape=(oshape, jax.ShapeDtypeStruct(oshape.shape, bf16)) if q_slabs else oshape,
        grid=(N_ALL // tm, ncols // tn),
        in_specs=[pl.BlockSpec((tm, D_MODEL), lambda i, j: (i, 0)),
                  pl.BlockSpec((1, D_MODEL), lambda i, j: (0, 0)),
                  pl.BlockSpec((1, 6, D_MODEL), lambda i, j: (mrow(i), 0, 0)),
                  pl.BlockSpec((D_MODEL, tn), lambda i, j: (0, j))],
        out_specs=(ospec, ospec) if q_slabs else ospec,
        scratch_shapes=[pltpu.VMEM((tm, D_MODEL), bf16)],
        compiler_params=_cparams(("arbitrary", "arbitrary")),
        name="norm_mod_project",
    )(x, g_pre.reshape(1, D_MODEL), mod_l, w_bf16)


def _outproj_body(o1_ref, o2_ref, w_ref, x_ref, g_ref, m_ref, out_ref, *, tm, sub):
    half = D_MODEL // 2
    out_ref[...] = (jnp.dot(o1_ref[...], w_ref[0:half, :], preferred_element_type=f32)
                    + jnp.dot(o2_ref[...], w_ref[half:D_MODEL, :], preferred_element_type=f32))
    _post_norm_residual_rows(out_ref, x_ref, g_ref, m_ref, tm, sub, 2)


def out_project_residual(o1, o2, w_bf16, x, g_post, mod_l):
    tm, sub = 512, NORM_ROWS
    half = D_MODEL // 2
    mrow = _mod_row(tm)
    return pl.pallas_call(
        functools.partial(_outproj_body, tm=tm, sub=sub),
        out_shape=jax.ShapeDtypeStruct((N_ALL, D_MODEL), f32),
        grid=(N_ALL // tm,),
        in_specs=[pl.BlockSpec((tm, half), lambda i: (i, 0)),
                  pl.BlockSpec((tm, half), lambda i: (i, 0)),
                  pl.BlockSpec((D_MODEL, D_MODEL), lambda i: (0, 0)),
                  pl.BlockSpec((tm, D_MODEL), lambda i: (i, 0)),
                  pl.BlockSpec((1, D_MODEL), lambda i: (0, 0)),
                  pl.BlockSpec((1, 6, D_MODEL), lambda i: (mrow(i), 0, 0))],
        out_specs=pl.BlockSpec((tm, D_MODEL), lambda i: (i, 0)),
        compiler_params=_cparams(("arbitrary",)),
        name="out_project_residual",
    )(o1, o2, w_bf16, x, g_post.reshape(1, D_MODEL), mod_l)


def _mlp_body(x_ref, gpre_ref, gpost_ref, m_ref, wu_ref, wd_ref, out_ref, h_ref, *, tm, sub, nj):
    j = pl.program_id(1)

    @pl.when(j == 0)
    def _():
        _norm_mod_rows(x_ref, gpre_ref, m_ref, h_ref, tm, sub, 3, 4)

    u = jnp.dot(h_ref[...], wu_ref[...], preferred_element_type=f32)
    u = jnp.maximum(u, 0.0)
    u = (u * u).astype(bf16)
    part = jnp.dot(u, wd_ref[...], preferred_element_type=f32)

    @pl.when(j == 0)
    def _():
        out_ref[...] = part

    @pl.when(j > 0)
    def _():
        out_ref[...] += part

    @pl.when(j == nj - 1)
    def _():
        _post_norm_residual_rows(out_ref, x_ref, gpost_ref, m_ref, tm, sub, 5)


def mlp_residual(x, g_pre, g_post, mod_l, wu_bf16, wd_bf16):
    tm, tf, sub = 512, 1024, NORM_ROWS
    nj = FF // tf
    mrow = _mod_row(tm)
    return pl.pallas_call(
        functools.partial(_mlp_body, tm=tm, sub=sub, nj=nj),
        out_shape=jax.ShapeDtypeStruct((N_ALL, D_MODEL), f32),
        grid=(N_ALL // tm, nj),
        in_specs=[pl.BlockSpec((tm, D_MODEL), lambda i, j: (i, 0)),
                  pl.BlockSpec((1, D_MODEL), lambda i, j: (0, 0)),
                  pl.BlockSpec((1, D_MODEL), lambda i, j: (0, 0)),
                  pl.BlockSpec((1, 6, D_MODEL), lambda i, j: (mrow(i), 0, 0)),
                  pl.BlockSpec((D_MODEL, tf), lambda i, j: (0, j)),
                  pl.BlockSpec((tf, D_MODEL), lambda i, j: (j, 0))],
        out_specs=pl.BlockSpec((tm, D_MODEL), lambda i, j: (i, 0)),
        scratch_shapes=[pltpu.VMEM((tm, D_MODEL), bf16)],
        compiler_params=_cparams(("arbitrary", "arbitrary")),
        name="mlp_residual",
    )(x, g_pre.reshape(1, D_MODEL), g_post.reshape(1, D_MODEL), mod_l, wu_bf16, wd_bf16)


def _rope_tables():
    nf = HD // 4
    t = jnp.arange(DEC_SEQ)
    row = (t // GRID_W).astype(f32)
    col = (t % GRID_W).astype(f32)
    inv = jnp.exp(-math.log(ROPE_BASE) * jnp.arange(nf, dtype=f32) / nf)
    lane = np.arange(HD)
    first_half = (lane % (2 * nf)) < nf
    pos = jnp.where(jnp.asarray(lane < HD // 2)[None, :], row[:, None], col[:, None])
    ang = pos * inv[jnp.asarray(lane % nf)][None, :]
    cos, sin = jnp.cos(ang), jnp.sin(ang)
    fh = jnp.asarray(first_half)[None, :]
    sin_a = jnp.where(fh, -sin, 0.0)
    sin_b = jnp.where(fh, 0.0, sin)

    half = (2, DEC_SEQ // 2, HD)
    return cos.reshape(half), sin_a.reshape(half), sin_b.reshape(half)


def _rope_body(p_ref, cos_ref, sa_ref, sb_ref, o_ref):
    t = pl.program_id(1) % 2
    scale = jnp.where(pl.program_id(0) < HEADS, ATTN_SCALE, 1.0).astype(f32)
    x = p_ref[0]
    y = x * cos_ref[t] + pltpu.roll(x, 96, 1) * sa_ref[t] + pltpu.roll(x, 32, 1) * sb_ref[t]
    o_ref[0] = (y * scale).astype(bf16)


def rope_lat(p_att, tables):
    tm = DEC_SEQ // 2
    ctx_tiles = N_CTX // tm
    cos, sa, sb = tables
    tspec = pl.BlockSpec((2, tm, HD), lambda s, i: (0, 0, 0))
    return pl.pallas_call(
        _rope_body,
        out_shape=jax.ShapeDtypeStruct((HEADS + A_KV_HEADS, N_LAT, HD), bf16),
        grid=(HEADS + A_KV_HEADS, N_LAT // tm),
        in_specs=[pl.BlockSpec((1, tm, HD), lambda s, i: (jnp.where(s < HEADS, s, s + 24), ctx_tiles + i, 0)),
                  tspec, tspec, tspec],
        out_specs=pl.BlockSpec((1, tm, HD), lambda s, i: (s, i, 0)),
        compiler_params=_cparams(("arbitrary", "arbitrary")),
        name="rope_lat",
    )(p_att, cos, sa, sb)


def _ctx_attn_body(*refs, reps, use_sink):
    if use_sink:
        sink_ref, q_ref, k_ref, v_ref, o_ref = refs
    else:
        q_ref, k_ref, v_ref, o_ref = refs
    g = pl.program_id(1)
    k = k_ref[0]
    v = v_ref[0]
    for r in range(reps):
        s = lax.dot_general(q_ref[r], k, NT, preferred_element_type=f32)
        m = jnp.max(s, axis=-1, keepdims=True)
        if use_sink:
            sk = sink_ref[g, r]
            m = jnp.maximum(m, sk)
        p = jnp.exp(s - m)
        den = jnp.sum(p, axis=-1, keepdims=True)
        if use_sink:
            den = den + jnp.exp(sk - m)
        o = jnp.dot(p.astype(bf16), v, preferred_element_type=f32) / den
        o_ref[:, r * HD:(r + 1) * HD] = o.astype(bf16)


def ctx_attention_a(pb, sink):
    return pl.pallas_call(
        functools.partial(_ctx_attn_body, reps=A_REP, use_sink=True),
        out_shape=jax.ShapeDtypeStruct((N_CTX, HEADS * HD), bf16),
        grid=(BATCH, A_KV_HEADS),
        in_specs=[pl.BlockSpec(memory_space=pltpu.SMEM),
                  pl.BlockSpec((A_REP, SEQ, HD), lambda b, g: (g, b, 0)),
                  pl.BlockSpec((1, SEQ, HD), lambda b, g: (32 + g, b, 0)),
                  pl.BlockSpec((1, SEQ, HD), lambda b, g: (34 + g, b, 0))],
        out_specs=pl.BlockSpec((SEQ, A_REP * HD), lambda b, g: (b, g)),
        compiler_params=_cparams(("arbitrary", "arbitrary")),
        name="ctx_attention_a",
    )(sink, pb, pb, pb)


def ctx_attention_b(pb):
    return pl.pallas_call(
        functools.partial(_ctx_attn_body, reps=1, use_sink=False),
        out_shape=jax.ShapeDtypeStruct((N_CTX, HEADS * HD), bf16),
        grid=(BATCH, HEADS),
        in_specs=[pl.BlockSpec((1, SEQ, HD), lambda b, h: (8 + h, b, 0)),
                  pl.BlockSpec((1, SEQ, HD), lambda b, h: (16 + h, b, 0)),
                  pl.BlockSpec((1, SEQ, HD), lambda b, h: (24 + h, b, 0))],
        out_specs=pl.BlockSpec((SEQ, HD), lambda b, h: (b, h)),
        compiler_params=_cparams(("arbitrary", "arbitrary")),
        name="ctx_attention_b",
    )(pb, pb, pb)


def _window_body(sink_ref, q_ref, k0_ref, k1_ref, k2_ref, v0_ref, v1_ref, v2_ref, kc_ref, vc_ref, o_ref, *, nb):
    g = pl.program_id(1)
    n = pl.program_id(2)
    rows = A_REP * WINDOW
    q = q_ref[...].reshape(rows, HD)
    kc = kc_ref[0, 0].astype(bf16)
    vc = vc_ref[0, 0].astype(bf16)
    s0 = lax.dot_general(q, k0_ref[0], NT, preferred_element_type=f32)
    s1 = lax.dot_general(q, k1_ref[0], NT, preferred_element_type=f32)
    s2 = lax.dot_general(q, k2_ref[0], NT, preferred_element_type=f32)
    sc = lax.dot_general(q, kc, NT, preferred_element_type=f32)
    qi = lax.broadcasted_iota(jnp.int32, (rows, WINDOW), 0) % WINDOW
    kj = lax.broadcasted_iota(jnp.int32, (rows, WINDOW), 1)
    neg = -jnp.inf
    s0 = jnp.where(kj >= qi, s0, neg)
    s0 = jnp.where(n > 0, s0, neg)
    s2 = jnp.where(kj <= qi, s2, neg)
    s2 = jnp.where(n < nb - 1, s2, neg)
    rep = lax.broadcasted_iota(jnp.int32, (rows, 1), 0) // WINDOW
    sk = jnp.zeros((rows, 1), f32)
    for r in range(A_REP):
        sk = jnp.where(rep == r, sink_ref[g, r], sk)
    m = jnp.maximum(jnp.maximum(jnp.max(s0, axis=-1, keepdims=True), jnp.max(s1, axis=-1, keepdims=True)),
                    jnp.maximum(jnp.max(s2, axis=-1, keepdims=True), jnp.max(sc, axis=-1, keepdims=True)))
    m = jnp.maximum(m, sk)
    p0 = jnp.exp(s0 - m)
    p1 = jnp.exp(s1 - m)
    p2 = jnp.exp(s2 - m)
    pc = jnp.exp(sc - m)
    den = (jnp.sum(p0, axis=-1, keepdims=True) + jnp.sum(p1, axis=-1, keepdims=True)
           + jnp.sum(p2, axis=-1, keepdims=True) + jnp.sum(pc, axis=-1, keepdims=True) + jnp.exp(sk - m))
    o = (jnp.dot(p0.astype(bf16), v0_ref[0], preferred_element_type=f32)
         + jnp.dot(p1.astype(bf16), v1_ref[0], preferred_element_type=f32)
         + jnp.dot(p2.astype(bf16), v2_ref[0], preferred_element_type=f32)
         + jnp.dot(pc.astype(bf16), vc, preferred_element_type=f32)) / den
    for r in range(A_REP):
        o_ref[:, r * HD:(r + 1) * HD] = o[r * WINDOW:(r + 1) * WINDOW].astype(bf16)


def window_attention_lat(pr, pb, sink, ka_c, va_c, j):
    nb = DEC_SEQ // WINDOW

    def qrow(b, n):
        return b * nb + n

    def kspec(slab0, shift, base=0):
        return pl.BlockSpec((1, WINDOW, HD),
                            lambda b, g, n: (slab0 + g, base + qrow(b, jnp.clip(n + shift, 0, nb - 1)), 0))

    vbase = N_CTX // WINDOW

    cspec = pl.BlockSpec((1, 1, PAST_LEN, HD), lambda b, g, n: (b, j, 0, g))
    return pl.pallas_call(
        functools.partial(_window_body, nb=nb),
        out_shape=jax.ShapeDtypeStruct((N_LAT, HEADS * HD), bf16),
        grid=(DEC_BATCH, A_KV_HEADS, nb),
        in_specs=[pl.BlockSpec(memory_space=pltpu.SMEM),
                  pl.BlockSpec((A_REP, WINDOW, HD), lambda b, g, n: (g, qrow(b, n), 0)),
                  kspec(HEADS, -1), kspec(HEADS, 0), kspec(HEADS, 1),
                  kspec(34, -1, vbase), kspec(34, 0, vbase), kspec(34, 1, vbase),
                  cspec, cspec],
        out_specs=pl.BlockSpec((WINDOW, A_REP * HD), lambda b, g, n: (b * nb + n, g)),
        compiler_params=_cparams(("arbitrary", "arbitrary", "arbitrary")),
        name="window_attention",
    )(sink, pr, pr, pr, pr, pb, pb, pb, ka_c, va_c)


NA_QROWS = 2
NA_WROWS = NA_QROWS + NA_ROWS - 1
NA_WIN_MAX = GRID_ROWS - NA_WROWS
NA_OFFS = (0, -2, -4, -5, -7)
NA_Q = NA_QROWS * GRID_W
NA_K = NA_WROWS * GRID_W


def _na_bias_tables(rel_bias):
    r0_of = {0: 0, -2: 2, -4: 4, -5: 28, -7: 30}
    cq = np.arange(GRID_W)[:, None]
    ck = np.arange(GRID_W)[None, :]
    wstart = np.clip(cq - NA_COLS // 2, 0, GRID_W - NA_COLS)
    col_ok = (ck >= wstart) & (ck < wstart + NA_COLS)
    dc = np.clip(ck - cq + NA_COLS - 1, 0, 2 * NA_COLS - 2)
    pick = ((dc[None] == np.arange(2 * NA_COLS - 1)[:, None, None]) & col_ok[None]).astype(np.float32)
    toep = jnp.einsum("hrd,dqk->hrqk", rel_bias, jnp.asarray(pick), precision=HI)
    toep = jnp.where(jnp.asarray(col_ok)[None, None], toep, -1e30)
    masked = jnp.full((HEADS, GRID_W, GRID_W), -1e30, f32)
    tables = []
    for off in NA_OFFS:
        r0 = r0_of[off]
        ws = int(np.clip(r0 - NA_ROWS // 2, 0, NA_WIN_MAX))
        assert ws - r0 == off
        qrows = []
        for a in range(NA_QROWS):
            r = r0 + a
            rs = int(np.clip(r - NA_ROWS // 2, 0, GRID_ROWS - NA_ROWS))
            blocks = []
            for w in range(NA_WROWS):
                krow = ws + w
                blocks.append(toep[:, krow - r + NA_ROWS - 1] if rs <= krow < rs + NA_ROWS else masked)
            qrows.append(jnp.concatenate(blocks, axis=2))
        tables.append(jnp.concatenate(qrows, axis=1))
    return jnp.stack(tables).astype(f32)


def _na_body(bias_ref, q_ref, k_ref, v_ref, kc_ref, vc_ref, o_ref, kcb_ref, vcb_ref):
    rb = pl.program_id(1)

    @pl.when(rb == 0)
    def _():
        kcb_ref[...] = kc_ref[0, 0].astype(bf16)
        vcb_ref[...] = vc_ref[0, 0].astype(bf16)

    ws = jnp.clip(NA_QROWS * rb - NA_ROWS // 2, 0, NA_WIN_MAX)
    win = pl.ds(pl.multiple_of(ws * GRID_W, GRID_W), NA_K)
    for h in range(HEADS):
        q = q_ref[h]
        lanes = slice(h * HD, (h + 1) * HD)
        sl = lax.dot_general(q, k_ref[h, win, :], NT, preferred_element_type=f32) + bias_ref[0, h]
        sc = lax.dot_general(q, kcb_ref[:, lanes], NT, preferred_element_type=f32)
        m = jnp.maximum(jnp.max(sl, axis=-1, keepdims=True), jnp.max(sc, axis=-1, keepdims=True))
        p_l = jnp.exp(sl - m)
        p_c = jnp.exp(sc - m)
        den = jnp.sum(p_l, axis=-1, keepdims=True) + jnp.sum(p_c, axis=-1, keepdims=True)
        o = (jnp.dot(p_l.astype(bf16), v_ref[h, win, :], preferred_element_type=f32)
             + jnp.dot(p_c.astype(bf16), vcb_ref[:, lanes], preferred_element_type=f32)) / den
        o_ref[:, lanes] = o.astype(bf16)


def neighbourhood_attention_lat(pb, bias_tab, kb_c, vb_c, j):
    nrb = GRID_ROWS // NA_QROWS
    base = N_CTX // NA_Q
    per_b = DEC_SEQ // NA_Q
    seq_base = N_CTX // DEC_SEQ

    def tbl(rb):
        return jnp.where(rb < 2, rb, jnp.where(rb < nrb - 2, 2, rb - (nrb - 5)))

    cspec = pl.BlockSpec((1, 1, PAST_LEN, HEADS * HD), lambda b, rb: (b, j, 0, 0))
    return pl.pallas_call(
        _na_body,
        out_shape=jax.ShapeDtypeStruct((N_LAT, HEADS * HD), bf16),
        grid=(DEC_BATCH, nrb),
        in_specs=[pl.BlockSpec((1, HEADS, NA_Q, NA_K), lambda b, rb: (tbl(rb), 0, 0, 0)),
                  pl.BlockSpec((HEADS, NA_Q, HD), lambda b, rb: (1, base + b * per_b + rb, 0)),
                  pl.BlockSpec((HEADS, DEC_SEQ, HD), lambda b, rb: (2, seq_base + b, 0)),
                  pl.BlockSpec((HEADS, DEC_SEQ, HD), lambda b, rb: (3, seq_base + b, 0)),
                  cspec, cspec],
        out_specs=pl.BlockSpec((NA_Q, HEADS * HD), lambda b, rb: (b * per_b + rb, 0)),
        scratch_shapes=[pltpu.VMEM((PAST_LEN, HEADS * HD), bf16), pltpu.VMEM((PAST_LEN, HEADS * HD), bf16)],
        compiler_params=_cparams(("arbitrary", "arbitrary")),
        name="neighbourhood_attention",
    )(bias_tab, pb, pb, pb, kb_c, vb_c)


def _gates_body(p_ref, par_ref, o_ref, *, tm, blk):
    lane = lax.broadcasted_iota(jnp.int32, (blk, LANES), 1)
    ii = lax.broadcasted_iota(jnp.int32, (blk, blk), 0)
    jj = lax.broadcasted_iota(jnp.int32, (blk, blk), 1)
    same = (ii // CHUNK) == (jj // CHUNK)
    lower = jnp.where(same & (jj <= ii), 1.0, 0.0).astype(f32)
    upper = jnp.where(same & (jj >= ii), 1.0, 0.0).astype(f32)
    cumulated = ((lane >= 16) & (lane < 32)) | ((lane >= 48) & (lane < 64))
    backward = (lane % 16) >= 8
    for r in range(tm // blk):
        x = p_ref[0, r * blk:(r + 1) * blk, :]
        z = x + par_ref[1:2, :]
        act = jnp.where(lane < 16, _sigmoid(z),
                        jnp.where(lane < 32, par_ref[0:1, :] * _softplus(z), jnp.where(lane < 48, z, -_softplus(-z))))
        pre = jnp.dot(lower, act, precision=HI, preferred_element_type=f32)
        suf = jnp.dot(upper, act, precision=HI, preferred_element_type=f32)
        o_ref[r * blk:(r + 1) * blk, :] = jnp.where(cumulated, jnp.where(backward, suf, pre), act)


def gate_activations(p_rec, a_log, dt_bias, b_i, b_f):
    tm = 1024
    zeros16 = jnp.zeros((16,), f32)
    neg_a = jnp.concatenate([zeros16, -jnp.exp(a_log.reshape(16).astype(f32)), zeros16, zeros16, jnp.zeros((64,), f32)])
    bias = jnp.concatenate([zeros16, dt_bias.reshape(16).astype(f32), b_i.reshape(16).astype(f32),
                            b_f.reshape(16).astype(f32), jnp.zeros((64,), f32)])
    par = jnp.concatenate([neg_a[None], bias[None], jnp.zeros((6, LANES), f32)], axis=0)
    return pl.pallas_call(
        functools.partial(_gates_body, tm=tm, blk=4 * CHUNK),
        out_shape=jax.ShapeDtypeStruct((N_ALL, LANES), f32),
        grid=(N_ALL // tm,),
        in_specs=[pl.BlockSpec((1, tm, LANES), lambda i: (64, i, 0)),
                  pl.BlockSpec((8, LANES), lambda i: (0, 0))],
        out_specs=pl.BlockSpec((tm, LANES), lambda i: (i, 0)),
        compiler_params=_cparams(("arbitrary",)),
        name="gate_activations",
    )(p_rec, par)


def _tri(reverse, strict):
    ii = lax.broadcasted_iota(jnp.int32, (CHUNK, CHUNK), 0)
    jj = lax.broadcasted_iota(jnp.int32, (CHUNK, CHUNK), 1)
    if reverse:
        return (jj > ii) if strict else (jj >= ii)
    return (jj < ii) if strict else (jj <= ii)


def _split(x):
    hi = x.astype(bf16)
    return hi, (x - hi.astype(f32)).astype(bf16)


def _mm(a, b):
    return jnp.dot(a.astype(bf16), b.astype(bf16), preferred_element_type=f32)


def _unit_tri_inverses(mats):
    ii = lax.broadcasted_iota(jnp.int32, (CHUNK, CHUNK), 0)
    jj = lax.broadcasted_iota(jnp.int32, (CHUNK, CHUNK), 1)
    same16 = (ii // 16) == (jj // 16)
    same32 = (ii // 32) == (jj // 32)
    eye = jnp.where(ii == jj, 1.0, 0.0).astype(f32)
    m = [jnp.where(same16, -a, 0.0) for a in mats]
    l1 = [jnp.where(same16, 0.0, jnp.where(same32, a, 0.0)) for a in mats]
    l2 = [jnp.where(same32, 0.0, a) for a in mats]
    p = [eye + x for x in m]
    m = [_mm(x, x) for x in m]
    for _ in range(2):
        r = [_mm(jnp.concatenate([x, y], axis=0), x) for x, y in zip(m, p)]
        m = [x[:CHUNK] for x in r]
        p = [y + x[CHUNK:] for x, y in zip(r, p)]
    p = [y + _mm(y, x) for x, y in zip(m, p)]
    p = [y - _mm(y, _mm(l, y)) for l, y in zip(l1, p)]
    return [y - _mm(y, _mm(l, y)) for l, y in zip(l2, p)]


def _gate_cols(gt_ref, rows, lane_ids):
    x = gt_ref[rows, :]
    lane = lax.broadcasted_iota(jnp.int32, x.shape, 1)
    return [jnp.sum(jnp.where(lane == i, x, 0.0), axis=-1, keepdims=True) for i in lane_ids]


PREP_CHUNKS = 4
HPS_CTX = 4
HPS_LAT = 2


def _chunk_rows(c):
    return pl.ds(pl.multiple_of(c * CHUNK, CHUNK), CHUNK)


def _chunk_pair_rows(c):
    return pl.ds(pl.multiple_of(c * 2 * CHUNK, 2 * CHUNK), 2 * CHUNK)


def _chunk_stat_rows(c):
    return pl.ds(pl.multiple_of(c * 8, 8), 8)


def _conv_silu(x_ref, cw_ref, T):
    x = x_ref[...]
    w = cw_ref[...]
    row = lax.broadcasted_iota(jnp.int32, (T, HD), 0)
    xp = jnp.where(row == 0, 0.0, pltpu.roll(x, 1, 0))
    xn = jnp.where(row == T - 1, 0.0, pltpu.roll(x, T - 1, 0))
    y = w[0:1] * xp + w[1:2] * x + w[2:3] * xn
    return y * _sigmoid(y)


def _delta_prepare(head, xq_ref, xk_ref, xv_ref, cwq_ref, cwk_ref, cwv_ref, gt_ref, gcum_rows,
                   q_s, k_s, v_s, u_s, wq_s, kd_s, at_s, eg_s, T):
    nc = T // CHUNK
    q = _conv_silu(xq_ref, cwq_ref, T)
    q_s[...] = q * lax.rsqrt(jnp.sum(q * q, axis=-1, keepdims=True) + EPS) * ATTN_SCALE
    k = _conv_silu(xk_ref, cwk_ref, T)
    k_s[...] = k * lax.rsqrt(jnp.sum(k * k, axis=-1, keepdims=True) + EPS)
    v_s[...] = _conv_silu(xv_ref, cwv_ref, T)
    incl = (_tri(False, False), _tri(True, False))
    strict = (_tri(False, True), _tri(True, True))

    def prep_group(t, carry):
        chains = [(t * PREP_CHUNKS + i, d) for i in range(PREP_CHUNKS) for d in (0, 1)]
        ld = []
        for c, d in chains:
            rows = _chunk_rows(c)
            beta_c, gcum_c = _gate_cols(gt_ref, rows, (d * HEADS + head, 2 * HEADS + d * HEADS + head))
            ld.append((q_s[rows, :], k_s[rows, :], v_s[rows, :], gcum_rows[d][pl.ds(c, 1), :], gcum_c, beta_c))
        decay = [jnp.exp(jnp.where(incl[d], x[4] - x[3], -jnp.inf)) for (c, d), x in zip(chains, ld)]
        kb = [x[1] * x[5] for x in ld]
        a = [jnp.where(strict[d], lax.dot_general(y.astype(bf16), x[1].astype(bf16), NT,
                                                  preferred_element_type=f32) * dc, 0.0)
             for (c, d), x, y, dc in zip(chains, ld, kb, decay)]
        inv = _unit_tri_inverses(a)
        for (c, d), (q, k, v, gcum_r, gcum_c, beta_c), kbi, dc, t_inv in zip(chains, ld, kb, decay, inv):
            rows = _chunk_rows(c)
            eg_c = jnp.exp(gcum_c)
            rhs_hi, rhs_lo = _split(jnp.concatenate([v * beta_c, kbi * eg_c], axis=1))
            t_inv = t_inv.astype(bf16)
            uw = (jnp.dot(t_inv, rhs_hi, preferred_element_type=f32)
                  + jnp.dot(t_inv, rhs_lo, preferred_element_type=f32))
            u_s[d, rows, :] = uw[:, :HD]
            wq_s[d, _chunk_pair_rows(c), :] = jnp.concatenate([uw[:, HD:], q * eg_c], axis=0).astype(bf16)
            att = lax.dot_general(q.astype(bf16), k.astype(bf16), NT, preferred_element_type=f32) * dc
            at_s[d, rows, :] = att.astype(bf16)
            glast = gcum_r[:, 0:1] if d == 1 else gcum_r[:, CHUNK - 1:CHUNK]
            kd_s[d, rows, :] = (k * jnp.exp(glast - gcum_c)).astype(bf16)
            eg_s[d, _chunk_stat_rows(c), :] = jnp.broadcast_to(jnp.exp(glast), (8, HD))
        return carry

    lax.fori_loop(0, nc // PREP_CHUNKS, prep_group, 0)


def _delta_body(xq_ref, xk_ref, xv_ref, xz_ref, cwq_ref, cwk_ref, cwv_ref, gt_ref, gf_ref, gb_ref,
                s0_ref, cn_ref, o_ref, sout_ref,
                q_s, k_s, v_s, u_s, wq_s, kd_s, at_s, eg_s, o_s, *, T, hps):
    nc = T // CHUNK
    for hh in range(hps):
        _delta_prepare(pl.program_id(1) * hps + hh, xq_ref.at[hh], xk_ref.at[hh], xv_ref.at[hh],
                       cwq_ref.at[hh], cwk_ref.at[hh], cwv_ref.at[hh], gt_ref,
                       (gf_ref.at[0, hh], gb_ref.at[0, hh]),
                       q_s.at[hh], k_s.at[hh], v_s.at[hh], u_s.at[hh], wq_s.at[hh], kd_s.at[hh], at_s.at[hh],
                       eg_s.at[hh], T)

    chains = [(hh, d) for hh in range(hps) for d in (0, 1)]

    def rec_step(accumulate):
        def step(i, carry):
            cs = [i if d == 0 else nc - 1 - i for hh, d in chains]
            r = [jnp.dot(wq_s[hh, d, _chunk_pair_rows(c), :], s.astype(bf16), preferred_element_type=f32)
                 for (hh, d), c, s in zip(chains, cs, carry)]
            vn = [(u_s[hh, d, _chunk_rows(c), :] - x[:CHUNK]).astype(bf16) for (hh, d), c, x in zip(chains, cs, r)]
            o = [x[CHUNK:] + jnp.dot(at_s[hh, d, _chunk_rows(c), :], y, preferred_element_type=f32)
                 for (hh, d), c, x, y in zip(chains, cs, r, vn)]
            upd = [lax.dot_general(kd_s[hh, d, _chunk_rows(c), :], y, TN_, preferred_element_type=f32)
                   for (hh, d), c, y in zip(chains, cs, vn)]
            for (hh, d), c, x in zip(chains, cs, o):
                if accumulate:
                    o_s[hh, _chunk_rows(c), :] += x
                else:
                    o_s[hh, _chunk_rows(c), :] = x
            return tuple(s * eg_s[hh, d, _chunk_stat_rows(c), :][0:1] + x
                         for (hh, d), c, s, x in zip(chains, cs, carry, upd))
        return step

    init = tuple(s0_ref[0, d, hh] for hh in range(hps) for d in (0, 1))
    mid = lax.fori_loop(0, nc // 2, rec_step(False), init)
    fin = lax.fori_loop(nc // 2, nc, rec_step(True), mid)
    for hh in range(hps):
        sout_ref[0, 0, hh] = fin[2 * hh]
        sout_ref[0, 1, hh] = fin[2 * hh + 1]
        o = o_s[hh]
        o = o * lax.rsqrt(jnp.mean(o * o, axis=-1, keepdims=True) + EPS) * cn_ref[...]
        z = xz_ref[hh]
        o_ref[:, hh * HD:(hh + 1) * HD] = (o * (z * _sigmoid(z))).astype(bf16)


def delta_mixer(p_rec, gates, gates_t, conv_w, c_norm, s0, T, row0, hps):
    B = s0.shape[0]
    nc = T // CHUNK
    rb0 = row0 // T

    def xspec(slab0):
        return pl.BlockSpec((hps, T, HD), lambda b, h: (slab0 // hps + h, rb0 + b, 0))

    def cwspec(slab0):
        return pl.BlockSpec((hps, 3, HD), lambda b, h: (slab0 // hps + h, 0, 0))

    def gspec(col0):
        return pl.BlockSpec((1, hps, nc, CHUNK), lambda b, h: (b, col0 // hps + h, 0, 0))

    sspec = pl.BlockSpec((1, 2, hps, HD, HD), lambda b, h: (b, 0, h, 0, 0))
    o, sout = pl.pallas_call(
        functools.partial(_delta_body, T=T, hps=hps),
        out_shape=(jax.ShapeDtypeStruct((B * T, HEADS * HD), bf16),
                   jax.ShapeDtypeStruct((B, 2, HEADS, HD, HD), f32)),
        grid=(B, HEADS // hps),
        in_specs=[xspec(0), xspec(8), xspec(16), xspec(24), cwspec(0), cwspec(8), cwspec(16),
                  pl.BlockSpec((T, LANES), lambda b, h: (rb0 + b, 0)),
                  gspec(16), gspec(24), sspec,
                  pl.BlockSpec((1, HD), lambda b, h: (0, 0))],
        out_specs=(pl.BlockSpec((T, hps * HD), lambda b, h: (b, h)), sspec),
        scratch_shapes=[pltpu.VMEM((hps, T, HD), f32), pltpu.VMEM((hps, T, HD), f32), pltpu.VMEM((hps, T, HD), f32),
                        pltpu.VMEM((hps, 2, T, HD), f32), pltpu.VMEM((hps, 2, 2 * T, HD), bf16),
                        pltpu.VMEM((hps, 2, T, HD), bf16), pltpu.VMEM((hps, 2, T, CHUNK), bf16),
                        pltpu.VMEM((hps, 2, nc * 8, HD), f32), pltpu.VMEM((hps, T, HD), f32)],
        compiler_params=_cparams(("arbitrary", "arbitrary")),
        name="delta_mixer",
    )(p_rec, p_rec, p_rec, p_rec, conv_w, conv_w, conv_w, gates, gates_t, gates_t, s0,
      c_norm.reshape(1, HD))
    return o, sout


def _mlstm_prepare(head, m0, xq_ref, xk_ref, gt_ref, i_rows, b_rows, qk_s, kw_s, col_s, ds_s, T):
    nc = T // CHUNK
    incl = (_tri(False, False), _tri(True, False))
    lane = lax.broadcasted_iota(jnp.int32, (CHUNK, HD), 1)
    row8 = lax.broadcasted_iota(jnp.int32, (8, HD), 0)

    def prep_group(t, carry):
        chains = [(t * PREP_CHUNKS + i if d == 0 else nc - 1 - (t * PREP_CHUNKS + i), d)
                  for i in range(PREP_CHUNKS) for d in (0, 1)]
        ld = []
        for c, d in chains:
            rows = _chunk_rows(c)
            q = (xq_ref[rows, :] * ATTN_SCALE).astype(bf16)
            k = xk_ref[rows, :]
            i_r = i_rows[d][pl.ds(c, 1), :]
            b_r = b_rows[d][pl.ds(c, 1), :]
            i_c, b_c = _gate_cols(gt_ref, rows, (4 * HEADS + d * HEADS + head, 6 * HEADS + d * HEADS + head))
            dmat = jnp.where(incl[d], b_c - b_r + i_r, -jnp.inf)
            dmax = jnp.max(dmat, axis=-1, keepdims=True)
            qk = lax.dot_general(q, k.astype(bf16), NT, preferred_element_type=f32)
            ld.append((k, b_r, b_c, i_c, dmat, dmax, qk))
        m = list(carry)
        for (c, d), (k, b_r, b_c, i_c, dmat, dmax, qk) in zip(chains, ld):
            last = 0 if d == 1 else CHUNK - 1
            rows = _chunk_rows(c)
            inter = b_c + m[d]
            mt = jnp.maximum(inter, dmax)
            qkw = qk * jnp.exp(dmat - mt)
            m_new = mt[last:last + 1, :]
            b_last = b_r[:, last:last + 1]
            kw = k * jnp.exp(b_last - b_c + i_c - m_new)
            qk_s[d, rows, :] = qkw.astype(bf16)
            kw_s[d, rows, :] = kw.astype(bf16)
            col_s[d, rows, :] = jnp.where(lane == 0, jnp.exp(inter - mt),
                                          jnp.where(lane == 1, jnp.sum(qkw, axis=-1, keepdims=True), jnp.exp(-mt)))
            ds = jnp.broadcast_to(jnp.exp(b_last + m[d] - m_new), (8, HD))
            ksum = jnp.broadcast_to(jnp.sum(kw, axis=0, keepdims=True), (8, HD))
            ds_s[d, _chunk_stat_rows(c), :] = jnp.where(row8 == 0, ds, ksum)
            m[d] = m_new
        return tuple(m)

    return lax.fori_loop(0, nc // PREP_CHUNKS, prep_group, m0)


def _mlstm_body(m0_ref, xq_ref, xk_ref, xv_ref, xo_ref, gt_ref, if_ref, ib_ref, ff_ref, fb_ref, c0_ref, n0_ref,
                dn_ref, o_ref, cout_ref, nout_ref, mout_ref,
                qk_s, kw_s, col_s, ds_s, h_s, *, T, hps):
    nc = T // CHUNK
    b_id = pl.program_id(0)
    for hh in range(hps):
        head = pl.program_id(1) * hps + hh
        m0 = (jnp.full((1, 1), m0_ref[b_id, 0, head], f32), jnp.full((1, 1), m0_ref[b_id, 1, head], f32))
        mf, mb = _mlstm_prepare(head, m0, xq_ref.at[hh], xk_ref.at[hh], gt_ref,
                                (if_ref.at[0, hh], ib_ref.at[0, hh]), (ff_ref.at[0, hh], fb_ref.at[0, hh]),
                                qk_s.at[hh], kw_s.at[hh], col_s.at[hh], ds_s.at[hh], T)
        mout_ref[0, 0, hh] = jnp.broadcast_to(mf, (1, HD))
        mout_ref[0, 1, hh] = jnp.broadcast_to(mb, (1, HD))

    chains = [(hh, d) for hh in range(hps) for d in (0, 1)]

    def rec_step(accumulate):
        def step(i, carry):
            cs = [i if d == 0 else nc - 1 - i for hh, d in chains]
            cms, ns = carry[0::2], carry[1::2]
            q = [xq_ref[hh, _chunk_rows(c), :] * ATTN_SCALE for (hh, d), c in zip(chains, cs)]
            v = [xv_ref[hh, _chunk_rows(c), :].astype(bf16) for (hh, d), c in zip(chains, cs)]
            qc = [jnp.dot(x.astype(bf16), cm.astype(bf16), preferred_element_type=f32) for x, cm in zip(q, cms)]
            qv = [jnp.dot(qk_s[hh, d, _chunk_rows(c), :], y, preferred_element_type=f32)
                  for (hh, d), c, y in zip(chains, cs, v)]
            upd = [lax.dot_general(kw_s[hh, d, _chunk_rows(c), :], y, TN_, preferred_element_type=f32)
                   for (hh, d), c, y in zip(chains, cs, v)]
            out = []
            for (hh, d), c, x, a, b, u, cm, n in zip(chains, cs, q, qc, qv, upd, cms, ns):
                rows = _chunk_rows(c)
                cols = col_s[hh, d, rows, :]
                w_inter = cols[:, 0:1]
                den = w_inter * jnp.sum(x * n, axis=-1, keepdims=True) + cols[:, 1:2]
                h = (w_inter * a + b) / jnp.maximum(jnp.abs(den), cols[:, 2:3])
                if accumulate:
                    h_s[hh, rows, :] += h
                else:
                    h_s[hh, rows, :] = h
                dsn = ds_s[hh, d, _chunk_stat_rows(c), :]
                out.extend((dsn[0:1] * cm + u, dsn[0:1] * n + dsn[1:2]))
            return tuple(out)
        return step

    init = []
    for hh in range(hps):
        for d in (0, 1):
            init.extend((c0_ref[0, d, hh], n0_ref[0, d, hh]))
    mid = lax.fori_loop(0, nc // 2, rec_step(False), tuple(init))
    fin = lax.fori_loop(nc // 2, nc, rec_step(True), mid)
    for hh in range(hps):
        for d in (0, 1):
            j = 2 * (2 * hh + d)
            cout_ref[0, d, hh] = fin[j]
            nout_ref[0, d, hh] = fin[j + 1]
        h = h_s[hh]
        h = h * lax.rsqrt(jnp.mean(h * h, axis=-1, keepdims=True) + EPS) * dn_ref[...]
        o_ref[:, hh * HD:(hh + 1) * HD] = (h * _sigmoid(xo_ref[hh])).astype(bf16)


def mlstm_mixer(p_rec, gates, gates_t, d_norm, c0, n0, m0, T, row0, hps):
    B = c0.shape[0]
    nc = T // CHUNK
    rb0 = row0 // T

    def xspec(slab0):
        return pl.BlockSpec((hps, T, HD), lambda b, h: (slab0 // hps + h, rb0 + b, 0))

    def gspec(col0):
        return pl.BlockSpec((1, hps, nc, CHUNK), lambda b, h: (b, col0 // hps + h, 0, 0))

    cspec = pl.BlockSpec((1, 2, hps, HD, HD), lambda b, h: (b, 0, h, 0, 0))
    vspec = pl.BlockSpec((1, 2, hps, 1, HD), lambda b, h: (b, 0, h, 0, 0))
    o, cout, nout, mout = pl.pallas_call(
        functools.partial(_mlstm_body, T=T, hps=hps),
        out_shape=(jax.ShapeDtypeStruct((B * T, HEADS * HD), bf16),
                   jax.ShapeDtypeStruct((B, 2, HEADS, HD, HD), f32),
                   jax.ShapeDtypeStruct((B, 2, HEADS, 1, HD), f32),
                   jax.ShapeDtypeStruct((B, 2, HEADS, 1, HD), f32)),
        grid=(B, HEADS // hps),
        in_specs=[pl.BlockSpec(memory_space=pltpu.SMEM),
                  xspec(32), xspec(40), xspec(48), xspec(56),
                  pl.BlockSpec((T, LANES), lambda b, h: (rb0 + b, 0)),
                  gspec(32), gspec(40), gspec(48), gspec(56), cspec, vspec,
                  pl.BlockSpec((1, HD), lambda b, h: (0, 0))],
        out_specs=(pl.BlockSpec((T, hps * HD), lambda b, h: (b, h)), cspec, vspec, vspec),
        scratch_shapes=[pltpu.VMEM((hps, 2, T, CHUNK), bf16), pltpu.VMEM((hps, 2, T, HD), bf16),
                        pltpu.VMEM((hps, 2, T, HD), f32), pltpu.VMEM((hps, 2, nc * 8, HD), f32),
                        pltpu.VMEM((hps, T, HD), f32)],
        compiler_params=_cparams(("arbitrary", "arbitrary")),
        name="mlstm_mixer",
    )(m0, p_rec, p_rec, p_rec, p_rec, gates, gates_t, gates_t, gates_t, gates_t, c0,
      n0.reshape(B, 2, HEADS, 1, HD), d_norm.reshape(1, HD))
    return o, cout, nout, mout


def _att_weight(w):
    return jnp.concatenate([w[:, :1024], w[:, 1536:], w[:, 1024:1536]], axis=1).astype(bf16)


def _rec_weight(w):
    pad = jnp.zeros((D_MODEL, REC_SLABS * LANES - 8256), w.dtype)
    return jnp.concatenate([w[:, :4096], w[:, 4128:8224], w[:, 4096:4128], w[:, 8224:], pad], axis=1).astype(bf16)


def _gates_rows(gates, row0, B, T):
    g = gates[row0:row0 + B * T, :64].reshape(B, T // CHUNK, CHUNK, 64)
    return jnp.transpose(g, (0, 3, 1, 2))


def kernel(x_prompt, x_sample, cache_a_k, cache_a_v, cache_b_k, cache_b_v, state_c, state_d_c, state_d_n, state_d_m, c, c_ctx, w_mod, b_mod, g_mix_pre, g_mix_post, g_ffn_pre, g_ffn_post, w_in_att, w_in_rec, w_out, a_sink, b_rel_bias, c_conv, c_a_log, c_dt_bias, c_norm, d_b_i, d_b_f, d_norm, w_up, w_down):
    x = jnp.concatenate([x_prompt.reshape(N_CTX, D_MODEL), x_sample.reshape(N_LAT, D_MODEL)], axis=0)
    c_all = jnp.concatenate([c_ctx[None, :], c, jnp.zeros((MOD_ROWS - 1 - DEC_BATCH, D_MODEL), f32)], axis=0)
    mods = adaln_all(c_all, w_mod, b_mod).reshape(DEPTH, MOD_ROWS, 6, D_MODEL)
    tables = _rope_tables()

    ka_c = cache_a_k.reshape(DEC_BATCH, N_ATT, PAST_LEN, A_KV_HEADS * HD)
    va_c = cache_a_v.reshape(DEC_BATCH, N_ATT, PAST_LEN, A_KV_HEADS * HD)
    kb_c = cache_b_k.reshape(DEC_BATCH, N_ATT, PAST_LEN, HEADS * HD)
    vb_c = cache_b_v.reshape(DEC_BATCH, N_ATT, PAST_LEN, HEADS * HD)

    att_kv = []
    rec_states = []
    for l in range(DEPTH):
        j = l // 2
        mod_l = mods[l]
        if l % 2 == 0:
            p, pb = norm_mod_project(x, g_mix_pre[l], mod_l, _att_weight(w_in_att[j]), q_slabs=2 * HEADS)
            att_kv.append(p[16:36, :N_CTX])
            pr = rope_lat(p, tables)
            sink = a_sink[j].astype(f32)
            o1 = jnp.concatenate([ctx_attention_a(pb, sink), window_attention_lat(pr, pb, sink, ka_c, va_c, j)],
                                 axis=0)
            bias_tab = _na_bias_tables(b_rel_bias[j].astype(f32))
            o2 = jnp.concatenate([ctx_attention_b(pb), neighbourhood_attention_lat(pb, bias_tab, kb_c, vb_c, j)], axis=0)
        else:
            p = norm_mod_project(x, g_mix_pre[l], mod_l, _rec_weight(w_in_rec[j]))
            gates = gate_activations(p, c_a_log[j], c_dt_bias[j], d_b_i[j], d_b_f[j])
            g_ctx = _gates_rows(gates, 0, BATCH, SEQ)
            g_lat = _gates_rows(gates, N_CTX, DEC_BATCH, DEC_SEQ)
            conv_w = jnp.transpose(c_conv[j].astype(f32).reshape(3, 3 * HEADS, HD), (1, 0, 2))
            zc = jnp.zeros((BATCH, 2, HEADS, HD, HD), f32)
            oc_ctx, sc_ctx = delta_mixer(p, gates, g_ctx, conv_w, c_norm[j].astype(f32), zc, SEQ, 0, HPS_CTX)
            oc_lat, _ = delta_mixer(p, gates, g_lat, conv_w, c_norm[j].astype(f32), state_c[:, j].astype(f32),
                                    DEC_SEQ, N_CTX, HPS_LAT)
            od_ctx, cd_ctx, nd_ctx, md_ctx = mlstm_mixer(
                p, gates, g_ctx, d_norm[j].astype(f32), zc, jnp.zeros((BATCH, 2, HEADS, HD), f32),
                jnp.zeros((BATCH, 2, HEADS), f32), SEQ, 0, HPS_CTX)
            od_lat, _, _, _ = mlstm_mixer(
                p, gates, g_lat, d_norm[j].astype(f32), state_d_c[:, j].astype(f32), state_d_n[:, j].astype(f32),
                state_d_m[:, j].astype(f32), DEC_SEQ, N_CTX, HPS_LAT)
            rec_states.append((sc_ctx, cd_ctx, nd_ctx.reshape(BATCH, 2, HEADS, HD), md_ctx[:, :, :, 0, 0]))
            o1 = jnp.concatenate([oc_ctx, oc_lat], axis=0)
            o2 = jnp.concatenate([od_ctx, od_lat], axis=0)
        x = out_project_residual(o1, o2, w_out[l].astype(bf16), x, g_mix_post[l], mod_l)
        x = mlp_residual(x, g_ffn_pre[l], g_ffn_post[l], mod_l, w_up[l].astype(bf16), w_down[l].astype(bf16))

    y_prompt = x[:N_CTX].reshape(BATCH, SEQ, D_MODEL)
    y_sample = x[N_CTX:].reshape(DEC_BATCH, DEC_SEQ, D_MODEL)

    def kv(slab0, n):
        per = [jnp.transpose(a[slab0:slab0 + n].reshape(n, BATCH, SEQ, HD), (1, 2, 0, 3)) for a in att_kv]
        return jnp.stack(per, axis=1)

    new_a_k, new_a_v = kv(16, A_KV_HEADS), kv(18, A_KV_HEADS)
    new_b_k, new_b_v = kv(0, HEADS), kv(8, HEADS)
    new_state_c = jnp.stack([s[0] for s in rec_states], axis=1)
    new_state_d_c = jnp.stack([s[1] for s in rec_states], axis=1)
    new_state_d_n = jnp.stack([s[2] for s in rec_states], axis=1)
    new_state_d_m = jnp.stack([s[3] for s in rec_states], axis=1)
    return (y_prompt, y_sample, new_a_k, new_a_v, new_b_k, new_b_v, new_state_c, new_state_d_c, new_state_d_n,
            new_state_d_m)
```

```python
import functools
import math

import numpy as np
import jax
import jax.numpy as jnp
from jax import lax
from jax.experimental import pallas as pl
from jax.experimental.pallas import tpu as pltpu

f32 = jnp.float32
bf16 = jnp.bfloat16
HI = lax.Precision.HIGHEST

D_MODEL = 2048
BATCH = 16
SEQ = 256
DEPTH = 4
DEC_BATCH = 8
DEC_SEQ = 2048
PAST_LEN = 512
GRID_W = 64
HD = 128
HEADS = 8
A_KV_HEADS = 2
A_REP = 4
FF = 4 * D_MODEL
WINDOW = 128
NA_ROWS = 8
NA_COLS = 16
CHUNK = 64
ROPE_BASE = 10000.0
EPS = 1e-6
N_ATT = 2
N_REC = 2
ATTN_SCALE = HD ** -0.5
GRID_ROWS = DEC_SEQ // GRID_W

N_CTX = BATCH * SEQ
N_LAT = DEC_BATCH * DEC_SEQ
N_ALL = N_CTX + N_LAT
MOD_ROWS = 16

ATT_SLABS = 36
REC_SLABS = 66
LANES = 128
NORM_ROWS = 64
VMEM_LIMIT = 56 * 1024 * 1024

NT = (((1,), (1,)), ((), ()))
TN_ = (((0,), (0,)), ((), ()))


def _cparams(sem):
    return pltpu.CompilerParams(dimension_semantics=sem, vmem_limit_bytes=VMEM_LIMIT)


def _sigmoid(x):
    return 1.0 / (1.0 + jnp.exp(-x))


def _softplus(x):
    e = jnp.exp(-jnp.abs(x))
    u = 1.0 + e
    log1p = jnp.where(u == 1.0, e, jnp.log(u) * (e / (u - 1.0)))
    return jnp.maximum(x, 0.0) + log1p


def _mod_row(tm):
    ctx_tiles = N_CTX // tm
    per_batch = DEC_SEQ // tm
    return lambda i: jnp.where(i < ctx_tiles, 0, 1 + (i - ctx_tiles) // per_batch)


def _adaln_body(c_ref, w_ref, b_ref, o_ref):
    c = c_ref[...]
    s = (c * _sigmoid(c)).astype(bf16)
    o_ref[0] = jnp.dot(s, w_ref[0].astype(bf16), preferred_element_type=f32) + b_ref[0]


def adaln_all(c_all, w_mod, b_mod):
    tn = 1024
    return pl.pallas_call(
        _adaln_body,
        out_shape=jax.ShapeDtypeStruct((DEPTH, MOD_ROWS, 6 * D_MODEL), f32),
        grid=(DEPTH, 6 * D_MODEL // tn),
        in_specs=[pl.BlockSpec((MOD_ROWS, D_MODEL), lambda l, j: (0, 0)),
                  pl.BlockSpec((1, D_MODEL, tn), lambda l, j: (l, 0, j)),
                  pl.BlockSpec((1, 1, tn), lambda l, j: (l, 0, j))],
        out_specs=pl.BlockSpec((1, MOD_ROWS, tn), lambda l, j: (l, 0, j)),
        compiler_params=_cparams(("arbitrary", "arbitrary")),
        name="adaln",
    )(c_all, w_mod, b_mod.reshape(DEPTH, 1, 6 * D_MODEL))


def _norm_mod_rows(x_ref, g_ref, m_ref, h_ref, tm, sub, sh_row, sc_row):
    gain = g_ref[...] * (1.0 + m_ref[0, sc_row:sc_row + 1, :])
    sh = m_ref[0, sh_row:sh_row + 1, :]

    def body(r, carry):
        rows = pl.ds(pl.multiple_of(r * sub, sub), sub)
        x = x_ref[rows, :]
        ms = jnp.mean(x * x, axis=-1, keepdims=True)
        h_ref[rows, :] = (x * lax.rsqrt(ms + EPS) * gain + sh).astype(bf16)
        return carry

    lax.fori_loop(0, tm // sub, body, 0, unroll=2)


def _post_norm_residual_rows(y_ref, x_ref, g_ref, m_ref, tm, sub, gate_row):
    gain = g_ref[...] * m_ref[0, gate_row:gate_row + 1, :]

    def body(r, carry):
        rows = pl.ds(pl.multiple_of(r * sub, sub), sub)
        y = y_ref[rows, :]
        ms = jnp.mean(y * y, axis=-1, keepdims=True)
        y_ref[rows, :] = x_ref[rows, :] + y * lax.rsqrt(ms + EPS) * gain
        return carry

    lax.fori_loop(0, tm // sub, body, 0, unroll=2)


def _proj_body(x_ref, g_ref, m_ref, w_ref, o_ref, *rest, tm, tn, sub, q_slabs):
    h_ref = rest[-1]

    @pl.when(pl.program_id(1) == 0)
    def _():
        _norm_mod_rows(x_ref, g_ref, m_ref, h_ref, tm, sub, 0, 1)

    acc = jnp.dot(h_ref[...], w_ref[...], preferred_element_type=f32)
    for s in range(tn // LANES):
        o_ref[s] = acc[:, s * LANES:(s + 1) * LANES]
    if q_slabs:
        ob_ref = rest[0]
        slab0 = pl.program_id(1) * (tn // LANES)
        for s in range(tn // LANES):
            scale = jnp.where(slab0 + s < q_slabs, ATTN_SCALE, 1.0).astype(f32)
            ob_ref[s] = (acc[:, s * LANES:(s + 1) * LANES] * scale).astype(bf16)


def norm_mod_project(x, g_pre, mod_l, w_bf16, q_slabs=0):
    tm, tn, sub = 1024, 768, NORM_ROWS
    ncols = w_bf16.shape[1]
    mrow = _mod_row(tm)
    ospec = pl.BlockSpec((tn // LANES, tm, LANES), lambda i, j: (j, i, 0))
    oshape = jax.ShapeDtypeStruct((ncols // LANES, N_ALL, LANES), f32)
    return pl.pallas_call(
        functools.partial(_proj_body, tm=tm, tn=tn, sub=sub, q_slabs=q_slabs),
        out_shape=(oshape, jax.ShapeDtypeStruct(oshape.shape, bf16)) if q_slabs else oshape,
        grid=(N_ALL // tm, ncols // tn),
        in_specs=[pl.BlockSpec((tm, D_MODEL), lambda i, j: (i, 0)),
                  pl.BlockSpec((1, D_MODEL), lambda i, j: (0, 0)),
                  pl.BlockSpec((1, 6, D_MODEL), lambda i, j: (mrow(i), 0, 0)),
                  pl.BlockSpec((D_MODEL, tn), lambda i, j: (0, j))],
        out_specs=(ospec, ospec) if q_slabs else ospec,
        scratch_shapes=[pltpu.VMEM((tm, D_MODEL), bf16)],
        compiler_params=_cparams(("arbitrary", "arbitrary")),
        name="norm_mod_project",
    )(x, g_pre.reshape(1, D_MODEL), mod_l, w_bf16)


def _outproj_body(c1_ref, c2_ref, l1_ref, l2_ref, w_ref, x_ref, g_ref, m_ref, out_ref, *, tm, sub):
    half = D_MODEL // 2
    is_ctx = pl.program_id(0) < N_CTX // tm

    def project(o1_ref, o2_ref):
        out_ref[...] = (jnp.dot(o1_ref[...], w_ref[0:half, :], preferred_element_type=f32)
                        + jnp.dot(o2_ref[...], w_ref[half:D_MODEL, :], preferred_element_type=f32))

    @pl.when(is_ctx)
    def _():
        project(c1_ref, c2_ref)

    @pl.when(jnp.logical_not(is_ctx))
    def _():
        project(l1_ref, l2_ref)

    _post_norm_residual_rows(out_ref, x_ref, g_ref, m_ref, tm, sub, 2)


def out_project_residual(o_ctx, o_lat, w_bf16, x, g_post, mod_l):
    tm, sub = 512, NORM_ROWS
    half = D_MODEL // 2
    mrow = _mod_row(tm)
    ctx_tiles = N_CTX // tm
    cspec = pl.BlockSpec((tm, half), lambda i: (jnp.minimum(i, ctx_tiles - 1), 0))
    lspec = pl.BlockSpec((tm, half), lambda i: (jnp.maximum(i - ctx_tiles, 0), 0))
    return pl.pallas_call(
        functools.partial(_outproj_body, tm=tm, sub=sub),
        out_shape=jax.ShapeDtypeStruct((N_ALL, D_MODEL), f32),
        grid=(N_ALL // tm,),
        in_specs=[cspec, cspec, lspec, lspec,
                  pl.BlockSpec((D_MODEL, D_MODEL), lambda i: (0, 0)),
                  pl.BlockSpec((tm, D_MODEL), lambda i: (i, 0)),
                  pl.BlockSpec((1, D_MODEL), lambda i: (0, 0)),
                  pl.BlockSpec((1, 6, D_MODEL), lambda i: (mrow(i), 0, 0))],
        out_specs=pl.BlockSpec((tm, D_MODEL), lambda i: (i, 0)),
        compiler_params=_cparams(("arbitrary",)),
        name="out_project_residual",
    )(o_ctx[0], o_ctx[1], o_lat[0], o_lat[1], w_bf16, x, g_post.reshape(1, D_MODEL), mod_l)


def _mlp_body(x_ref, gpre_ref, gpost_ref, m_ref, wu_ref, wd_ref, out_ref, h_ref, *, tm, sub, nj):
    j = pl.program_id(1)

    @pl.when(j == 0)
    def _():
        _norm_mod_rows(x_ref, gpre_ref, m_ref, h_ref, tm, sub, 3, 4)

    u = jnp.dot(h_ref[...], wu_ref[...], preferred_element_type=f32)
    u = jnp.maximum(u, 0.0)
    u = (u * u).astype(bf16)
    part = jnp.dot(u, wd_ref[...], preferred_element_type=f32)

    @pl.when(j == 0)
    def _():
        out_ref[...] = part

    @pl.when(j > 0)
    def _():
        out_ref[...] += part

    @pl.when(j == nj - 1)
    def _():
        _post_norm_residual_rows(out_ref, x_ref, gpost_ref, m_ref, tm, sub, 5)


def mlp_residual(x, g_pre, g_post, mod_l, wu_bf16, wd_bf16):
    tm, tf, sub = 512, 1024, NORM_ROWS
    nj = FF // tf
    mrow = _mod_row(tm)
    return pl.pallas_call(
        functools.partial(_mlp_body, tm=tm, sub=sub, nj=nj),
        out_shape=jax.ShapeDtypeStruct((N_ALL, D_MODEL), f32),
        grid=(N_ALL // tm, nj),
        in_specs=[pl.BlockSpec((tm, D_MODEL), lambda i, j: (i, 0)),
                  pl.BlockSpec((1, D_MODEL), lambda i, j: (0, 0)),
                  pl.BlockSpec((1, D_MODEL), lambda i, j: (0, 0)),
                  pl.BlockSpec((1, 6, D_MODEL), lambda i, j: (mrow(i), 0, 0)),
                  pl.BlockSpec((D_MODEL, tf), lambda i, j: (0, j)),
                  pl.BlockSpec((tf, D_MODEL), lambda i, j: (j, 0))],
        out_specs=pl.BlockSpec((tm, D_MODEL), lambda i, j: (i, 0)),
        scratch_shapes=[pltpu.VMEM((tm, D_MODEL), bf16)],
        compiler_params=_cparams(("arbitrary", "arbitrary")),
        name="mlp_residual",
    )(x, g_pre.reshape(1, D_MODEL), g_post.reshape(1, D_MODEL), mod_l, wu_bf16, wd_bf16)


def _rope_tables():
    nf = HD // 4
    t = jnp.arange(DEC_SEQ)
    row = (t // GRID_W).astype(f32)
    col = (t % GRID_W).astype(f32)
    inv = jnp.exp(-math.log(ROPE_BASE) * jnp.arange(nf, dtype=f32) / nf)
    lane = np.arange(HD)
    first_half = (lane % (2 * nf)) < nf
    pos = jnp.where(jnp.asarray(lane < HD // 2)[None, :], row[:, None], col[:, None])
    ang = pos * inv[jnp.asarray(lane % nf)][None, :]
    cos, sin = jnp.cos(ang), jnp.sin(ang)
    fh = jnp.asarray(first_half)[None, :]
    sin_a = jnp.where(fh, -sin, 0.0)
    sin_b = jnp.where(fh, 0.0, sin)

    half = (2, DEC_SEQ // 2, HD)
    return cos.reshape(half), sin_a.reshape(half), sin_b.reshape(half)


def _rope_body(p_ref, cos_ref, sa_ref, sb_ref, o_ref):
    t = pl.program_id(1) % 2
    scale = jnp.where(pl.program_id(0) < HEADS, ATTN_SCALE, 1.0).astype(f32)
    x = p_ref[0]
    y = x * cos_ref[t] + pltpu.roll(x, 96, 1) * sa_ref[t] + pltpu.roll(x, 32, 1) * sb_ref[t]
    o_ref[0] = (y * scale).astype(bf16)


def rope_lat(p_att, tables):
    tm = DEC_SEQ // 2
    ctx_tiles = N_CTX // tm
    cos, sa, sb = tables
    tspec = pl.BlockSpec((2, tm, HD), lambda s, i: (0, 0, 0))
    return pl.pallas_call(
        _rope_body,
        out_shape=jax.ShapeDtypeStruct((HEADS + A_KV_HEADS, N_LAT, HD), bf16),
        grid=(HEADS + A_KV_HEADS, N_LAT // tm),
        in_specs=[pl.BlockSpec((1, tm, HD), lambda s, i: (jnp.where(s < HEADS, s, s + 24), ctx_tiles + i, 0)),
                  tspec, tspec, tspec],
        out_specs=pl.BlockSpec((1, tm, HD), lambda s, i: (s, i, 0)),
        compiler_params=_cparams(("arbitrary", "arbitrary")),
        name="rope_lat",
    )(p_att, cos, sa, sb)


def _ctx_attn_body(*refs, reps, use_sink):
    if use_sink:
        sink_ref, q_ref, k_ref, v_ref, o_ref = refs
    else:
        q_ref, k_ref, v_ref, o_ref = refs
    g = pl.program_id(1)
    k = k_ref[0]
    v = v_ref[0]
    for r in range(reps):
        s = lax.dot_general(q_ref[r], k, NT, preferred_element_type=f32)
        m = jnp.max(s, axis=-1, keepdims=True)
        if use_sink:
            sk = sink_ref[g, r]
            m = jnp.maximum(m, sk)
        p = jnp.exp(s - m)
        den = jnp.sum(p, axis=-1, keepdims=True)
        if use_sink:
            den = den + jnp.exp(sk - m)
        o = jnp.dot(p.astype(bf16), v, preferred_element_type=f32) / den
        o_ref[:, r * HD:(r + 1) * HD] = o.astype(bf16)


def ctx_attention_a(pb, sink):
    return pl.pallas_call(
        functools.partial(_ctx_attn_body, reps=A_REP, use_sink=True),
        out_shape=jax.ShapeDtypeStruct((N_CTX, HEADS * HD), bf16),
        grid=(BATCH, A_KV_HEADS),
        in_specs=[pl.BlockSpec(memory_space=pltpu.SMEM),
                  pl.BlockSpec((A_REP, SEQ, HD), lambda b, g: (g, b, 0)),
                  pl.BlockSpec((1, SEQ, HD), lambda b, g: (32 + g, b, 0)),
                  pl.BlockSpec((1, SEQ, HD), lambda b, g: (34 + g, b, 0))],
        out_specs=pl.BlockSpec((SEQ, A_REP * HD), lambda b, g: (b, g)),
        compiler_params=_cparams(("arbitrary", "arbitrary")),
        name="ctx_attention_a",
    )(sink, pb, pb, pb)


def ctx_attention_b(pb):
    return pl.pallas_call(
        functools.partial(_ctx_attn_body, reps=1, use_sink=False),
        out_shape=jax.ShapeDtypeStruct((N_CTX, HEADS * HD), bf16),
        grid=(BATCH, HEADS),
        in_specs=[pl.BlockSpec((1, SEQ, HD), lambda b, h: (8 + h, b, 0)),
                  pl.BlockSpec((1, SEQ, HD), lambda b, h: (16 + h, b, 0)),
                  pl.BlockSpec((1, SEQ, HD), lambda b, h: (24 + h, b, 0))],
        out_specs=pl.BlockSpec((SEQ, HD), lambda b, h: (b, h)),
        compiler_params=_cparams(("arbitrary", "arbitrary")),
        name="ctx_attention_b",
    )(pb, pb, pb)


def _window_body(sink_ref, q_ref, k0_ref, k1_ref, k2_ref, v0_ref, v1_ref, v2_ref, kc_ref, vc_ref, o_ref, *, nb):
    g = pl.program_id(1)
    n = pl.program_id(2)
    rows = A_REP * WINDOW
    q = q_ref[...].reshape(rows, HD)
    kc = kc_ref[0, 0].astype(bf16)
    vc = vc_ref[0, 0].astype(bf16)
    s0 = lax.dot_general(q, k0_ref[0], NT, preferred_element_type=f32)
    s1 = lax.dot_general(q, k1_ref[0], NT, preferred_element_type=f32)
    s2 = lax.dot_general(q, k2_ref[0], NT, preferred_element_type=f32)
    sc = lax.dot_general(q, kc, NT, preferred_element_type=f32)
    qi = lax.broadcasted_iota(jnp.int32, (rows, WINDOW), 0) % WINDOW
    kj = lax.broadcasted_iota(jnp.int32, (rows, WINDOW), 1)
    neg = -jnp.inf
    s0 = jnp.where(kj >= qi, s0, neg)
    s0 = jnp.where(n > 0, s0, neg)
    s2 = jnp.where(kj <= qi, s2, neg)
    s2 = jnp.where(n < nb - 1, s2, neg)
    rep = lax.broadcasted_iota(jnp.int32, (rows, 1), 0) // WINDOW
    sk = jnp.zeros((rows, 1), f32)
    for r in range(A_REP):
        sk = jnp.where(rep == r, sink_ref[g, r], sk)
    m = jnp.maximum(jnp.maximum(jnp.max(s0, axis=-1, keepdims=True), jnp.max(s1, axis=-1, keepdims=True)),
                    jnp.maximum(jnp.max(s2, axis=-1, keepdims=True), jnp.max(sc, axis=-1, keepdims=True)))
    m = jnp.maximum(m, sk)
    p0 = jnp.exp(s0 - m)
    p1 = jnp.exp(s1 - m)
    p2 = jnp.exp(s2 - m)
    pc = jnp.exp(sc - m)
    den = (jnp.sum(p0, axis=-1, keepdims=True) + jnp.sum(p1, axis=-1, keepdims=True)
           + jnp.sum(p2, axis=-1, keepdims=True) + jnp.sum(pc, axis=-1, keepdims=True) + jnp.exp(sk - m))
    o = (jnp.dot(p0.astype(bf16), v0_ref[0], preferred_element_type=f32)
         + jnp.dot(p1.astype(bf16), v1_ref[0], preferred_element_type=f32)
         + jnp.dot(p2.astype(bf16), v2_ref[0], preferred_element_type=f32)
         + jnp.dot(pc.astype(bf16), vc, preferred_element_type=f32)) / den
    for r in range(A_REP):
        o_ref[:, r * HD:(r + 1) * HD] = o[r * WINDOW:(r + 1) * WINDOW].astype(bf16)


def window_attention_lat(pr, pb, sink, ka_c, va_c, j):
    nb = DEC_SEQ // WINDOW

    def qrow(b, n):
        return b * nb + n

    def kspec(slab0, shift, base=0):
        return pl.BlockSpec((1, WINDOW, HD),
                            lambda b, g, n: (slab0 + g, base + qrow(b, jnp.clip(n + shift, 0, nb - 1)), 0))

    vbase = N_CTX // WINDOW

    cspec = pl.BlockSpec((1, 1, PAST_LEN, HD), lambda b, g, n: (b, j, 0, g))
    return pl.pallas_call(
        functools.partial(_window_body, nb=nb),
        out_shape=jax.ShapeDtypeStruct((N_LAT, HEADS * HD), bf16),
        grid=(DEC_BATCH, A_KV_HEADS, nb),
        in_specs=[pl.BlockSpec(memory_space=pltpu.SMEM),
                  pl.BlockSpec((A_REP, WINDOW, HD), lambda b, g, n: (g, qrow(b, n), 0)),
                  kspec(HEADS, -1), kspec(HEADS, 0), kspec(HEADS, 1),
                  kspec(34, -1, vbase), kspec(34, 0, vbase), kspec(34, 1, vbase),
                  cspec, cspec],
        out_specs=pl.BlockSpec((WINDOW, A_REP * HD), lambda b, g, n: (b * nb + n, g)),
        compiler_params=_cparams(("arbitrary", "arbitrary", "arbitrary")),
        name="window_attention",
    )(sink, pr, pr, pr, pr, pb, pb, pb, ka_c, va_c)


NA_QROWS = 2
NA_WROWS = NA_QROWS + NA_ROWS - 1
NA_WIN_MAX = GRID_ROWS - NA_WROWS
NA_OFFS = (0, -2, -4, -5, -7)
NA_HEAD_GROUP = 4
NA_Q = NA_QROWS * GRID_W
NA_K = NA_WROWS * GRID_W


def _na_bias_tables(rel_bias):
    r0_of = {0: 0, -2: 2, -4: 4, -5: 28, -7: 30}
    cq = np.arange(GRID_W)[:, None]
    ck = np.arange(GRID_W)[None, :]
    wstart = np.clip(cq - NA_COLS // 2, 0, GRID_W - NA_COLS)
    col_ok = (ck >= wstart) & (ck < wstart + NA_COLS)
    dc = np.clip(ck - cq + NA_COLS - 1, 0, 2 * NA_COLS - 2)
    pick = ((dc[None] == np.arange(2 * NA_COLS - 1)[:, None, None]) & col_ok[None]).astype(np.float32)
    toep = jnp.einsum("hrd,dqk->hrqk", rel_bias, jnp.asarray(pick), precision=HI)
    toep = jnp.where(jnp.asarray(col_ok)[None, None], toep, -1e30)
    masked = jnp.full((HEADS, GRID_W, GRID_W), -1e30, f32)
    tables = []
    for off in NA_OFFS:
        r0 = r0_of[off]
        ws = int(np.clip(r0 - NA_ROWS // 2, 0, NA_WIN_MAX))
        assert ws - r0 == off
        qrows = []
        for a in range(NA_QROWS):
            r = r0 + a
            rs = int(np.clip(r - NA_ROWS // 2, 0, GRID_ROWS - NA_ROWS))
            blocks = []
            for w in range(NA_WROWS):
                krow = ws + w
                blocks.append(toep[:, krow - r + NA_ROWS - 1] if rs <= krow < rs + NA_ROWS else masked)
            qrows.append(jnp.concatenate(blocks, axis=2))
        tables.append(jnp.concatenate(qrows, axis=1))
    return jnp.stack(tables).astype(f32)


def _na_body(bias_ref, q_ref, k_ref, v_ref, kc_ref, vc_ref, o_ref, kcb_ref, vcb_ref):
    rb = pl.program_id(1)

    @pl.when(rb == 0)
    def _():
        kcb_ref[...] = kc_ref[0, 0].astype(bf16)
        vcb_ref[...] = vc_ref[0, 0].astype(bf16)

    ws = jnp.clip(NA_QROWS * rb - NA_ROWS // 2, 0, NA_WIN_MAX)
    win = pl.ds(pl.multiple_of(ws * GRID_W, GRID_W), NA_K)
    def lanes(h):
        return slice(h * HD, (h + 1) * HD)

    for h0 in range(0, HEADS, NA_HEAD_GROUP):
        hs = range(h0, h0 + NA_HEAD_GROUP)
        sl = [lax.dot_general(q_ref[h], k_ref[h, win, :], NT, preferred_element_type=f32) + bias_ref[0, h] for h in hs]
        sc = [lax.dot_general(q_ref[h], kcb_ref[:, lanes(h)], NT, preferred_element_type=f32) for h in hs]
        m = [jnp.maximum(jnp.max(a, axis=-1, keepdims=True), jnp.max(b, axis=-1, keepdims=True))
             for a, b in zip(sl, sc)]
        p_l = [jnp.exp(a - x) for a, x in zip(sl, m)]
        p_c = [jnp.exp(b - x) for b, x in zip(sc, m)]
        den = [jnp.sum(a, axis=-1, keepdims=True) + jnp.sum(b, axis=-1, keepdims=True) for a, b in zip(p_l, p_c)]
        o = [jnp.dot(a.astype(bf16), v_ref[h, win, :], preferred_element_type=f32)
             + jnp.dot(b.astype(bf16), vcb_ref[:, lanes(h)], preferred_element_type=f32)
             for h, a, b in zip(hs, p_l, p_c)]
        for h, x, d in zip(hs, o, den):
            o_ref[:, lanes(h)] = (x / d).astype(bf16)


def neighbourhood_attention_lat(pb, bias_tab, kb_c, vb_c, j):
    nrb = GRID_ROWS // NA_QROWS
    base = N_CTX // NA_Q
    per_b = DEC_SEQ // NA_Q
    seq_base = N_CTX // DEC_SEQ

    def tbl(rb):
        return jnp.where(rb < 2, rb, jnp.where(rb < nrb - 2, 2, rb - (nrb - 5)))

    cspec = pl.BlockSpec((1, 1, PAST_LEN, HEADS * HD), lambda b, rb: (b, j, 0, 0))
    return pl.pallas_call(
        _na_body,
        out_shape=jax.ShapeDtypeStruct((N_LAT, HEADS * HD), bf16),
        grid=(DEC_BATCH, nrb),
        in_specs=[pl.BlockSpec((1, HEADS, NA_Q, NA_K), lambda b, rb: (tbl(rb), 0, 0, 0)),
                  pl.BlockSpec((HEADS, NA_Q, HD), lambda b, rb: (1, base + b * per_b + rb, 0)),
                  pl.BlockSpec((HEADS, DEC_SEQ, HD), lambda b, rb: (2, seq_base + b, 0)),
                  pl.BlockSpec((HEADS, DEC_SEQ, HD), lambda b, rb: (3, seq_base + b, 0)),
                  cspec, cspec],
        out_specs=pl.BlockSpec((NA_Q, HEADS * HD), lambda b, rb: (b * per_b + rb, 0)),
        scratch_shapes=[pltpu.VMEM((PAST_LEN, HEADS * HD), bf16), pltpu.VMEM((PAST_LEN, HEADS * HD), bf16)],
        compiler_params=_cparams(("arbitrary", "arbitrary")),
        name="neighbourhood_attention",
    )(bias_tab, pb, pb, pb, kb_c, vb_c)


def _gates_body(p_ref, par_ref, o_ref, *, tm, blk):
    lane = lax.broadcasted_iota(jnp.int32, (blk, LANES), 1)
    ii = lax.broadcasted_iota(jnp.int32, (blk, blk), 0)
    jj = lax.broadcasted_iota(jnp.int32, (blk, blk), 1)
    same = (ii // CHUNK) == (jj // CHUNK)
    lower = jnp.where(same & (jj <= ii), 1.0, 0.0).astype(f32)
    upper = jnp.where(same & (jj >= ii), 1.0, 0.0).astype(f32)
    cumulated = ((lane >= 16) & (lane < 32)) | ((lane >= 48) & (lane < 64))
    backward = (lane % 16) >= 8
    for r in range(tm // blk):
        x = p_ref[0, r * blk:(r + 1) * blk, :]
        z = x + par_ref[1:2, :]
        act = jnp.where(lane < 16, _sigmoid(z),
                        jnp.where(lane < 32, par_ref[0:1, :] * _softplus(z), jnp.where(lane < 48, z, -_softplus(-z))))
        pre = jnp.dot(lower, act, precision=HI, preferred_element_type=f32)
        suf = jnp.dot(upper, act, precision=HI, preferred_element_type=f32)
        o_ref[r * blk:(r + 1) * blk, :] = jnp.where(cumulated, jnp.where(backward, suf, pre), act)


def gate_activations(p_rec, a_log, dt_bias, b_i, b_f):
    tm = 1024
    zeros16 = jnp.zeros((16,), f32)
    neg_a = jnp.concatenate([zeros16, -jnp.exp(a_log.reshape(16).astype(f32)), zeros16, zeros16, jnp.zeros((64,), f32)])
    bias = jnp.concatenate([zeros16, dt_bias.reshape(16).astype(f32), b_i.reshape(16).astype(f32),
                            b_f.reshape(16).astype(f32), jnp.zeros((64,), f32)])
    par = jnp.concatenate([neg_a[None], bias[None], jnp.zeros((6, LANES), f32)], axis=0)
    return pl.pallas_call(
        functools.partial(_gates_body, tm=tm, blk=4 * CHUNK),
        out_shape=jax.ShapeDtypeStruct((N_ALL, LANES), f32),
        grid=(N_ALL // tm,),
        in_specs=[pl.BlockSpec((1, tm, LANES), lambda i: (64, i, 0)),
                  pl.BlockSpec((8, LANES), lambda i: (0, 0))],
        out_specs=pl.BlockSpec((tm, LANES), lambda i: (i, 0)),
        compiler_params=_cparams(("arbitrary",)),
        name="gate_activations",
    )(p_rec, par)


def _tri(reverse, strict):
    ii = lax.broadcasted_iota(jnp.int32, (CHUNK, CHUNK), 0)
    jj = lax.broadcasted_iota(jnp.int32, (CHUNK, CHUNK), 1)
    if reverse:
        return (jj > ii) if strict else (jj >= ii)
    return (jj < ii) if strict else (jj <= ii)


def _split(x):
    hi = x.astype(bf16)
    return hi, (x - hi.astype(f32)).astype(bf16)


def _mm(a, b):
    return jnp.dot(a.astype(bf16), b.astype(bf16), preferred_element_type=f32)


def _unit_tri_inverses(mats):
    ii = lax.broadcasted_iota(jnp.int32, (CHUNK, CHUNK), 0)
    jj = lax.broadcasted_iota(jnp.int32, (CHUNK, CHUNK), 1)
    same16 = (ii // 16) == (jj // 16)
    same32 = (ii // 32) == (jj // 32)
    eye = jnp.where(ii == jj, 1.0, 0.0).astype(f32)
    m = [jnp.where(same16, -a, 0.0) for a in mats]
    l1 = [jnp.where(same16, 0.0, jnp.where(same32, a, 0.0)) for a in mats]
    l2 = [jnp.where(same32, 0.0, a) for a in mats]
    p = [eye + x for x in m]
    m = [_mm(x, x) for x in m]
    for _ in range(2):
        r = [_mm(jnp.concatenate([x, y], axis=0), x) for x, y in zip(m, p)]
        m = [x[:CHUNK] for x in r]
        p = [y + x[CHUNK:] for x, y in zip(r, p)]
    p = [y + _mm(y, x) for x, y in zip(m, p)]
    p = [y - _mm(y, _mm(l, y)) for l, y in zip(l1, p)]
    return [y - _mm(y, _mm(l, y)) for l, y in zip(l2, p)]


def _gate_cols(gt_ref, rows, lane_ids):
    x = gt_ref[rows, :]
    lane = lax.broadcasted_iota(jnp.int32, x.shape, 1)
    return [jnp.sum(jnp.where(lane == i, x, 0.0), axis=-1, keepdims=True) for i in lane_ids]


DELTA_PREP_CHUNKS = 8
MLSTM_PREP_CHUNKS = 4
HPS_CTX = 4
HPS_LAT = 2


def _chunk_rows(c):
    return pl.ds(pl.multiple_of(c * CHUNK, CHUNK), CHUNK)


def _chunk_pair_rows(c):
    return pl.ds(pl.multiple_of(c * 2 * CHUNK, 2 * CHUNK), 2 * CHUNK)


def _chunk_stat_rows(c):
    return pl.ds(pl.multiple_of(c * 8, 8), 8)


def _conv_silu(x_ref, cw_ref, T):
    x = x_ref[...]
    w = cw_ref[...]
    row = lax.broadcasted_iota(jnp.int32, (T, HD), 0)
    xp = jnp.where(row == 0, 0.0, pltpu.roll(x, 1, 0))
    xn = jnp.where(row == T - 1, 0.0, pltpu.roll(x, T - 1, 0))
    y = w[0:1] * xp + w[1:2] * x + w[2:3] * xn
    return y * _sigmoid(y)


def _delta_prepare(head, xq_ref, xk_ref, xv_ref, cwq_ref, cwk_ref, cwv_ref, gt_ref, gcum_rows,
                   q_s, k_s, v_s, u_s, wq_s, kd_s, at_s, eg_s, T):
    nc = T // CHUNK
    q = _conv_silu(xq_ref, cwq_ref, T)
    q_s[...] = q * lax.rsqrt(jnp.sum(q * q, axis=-1, keepdims=True) + EPS) * ATTN_SCALE
    k = _conv_silu(xk_ref, cwk_ref, T)
    k_s[...] = k * lax.rsqrt(jnp.sum(k * k, axis=-1, keepdims=True) + EPS)
    v_s[...] = _conv_silu(xv_ref, cwv_ref, T)
    incl = (_tri(False, False), _tri(True, False))
    strict = (_tri(False, True), _tri(True, True))
    group = min(DELTA_PREP_CHUNKS, nc)

    def prep_group(t, carry):
        chains = [(t * group + i, d) for i in range(group) for d in (0, 1)]
        ld = []
        for c, d in chains:
            rows = _chunk_rows(c)
            beta_c, gcum_c = _gate_cols(gt_ref, rows, (d * HEADS + head, 2 * HEADS + d * HEADS + head))
            ld.append((q_s[rows, :], k_s[rows, :], v_s[rows, :], gcum_rows[d][pl.ds(c, 1), :], gcum_c, beta_c))
        decay = [jnp.exp(jnp.where(incl[d], x[4] - x[3], -jnp.inf)) for (c, d), x in zip(chains, ld)]
        kb = [x[1] * x[5] for x in ld]
        a = [jnp.where(strict[d], lax.dot_general(y.astype(bf16), x[1].astype(bf16), NT,
                                                  preferred_element_type=f32) * dc, 0.0)
             for (c, d), x, y, dc in zip(chains, ld, kb, decay)]
        inv = _unit_tri_inverses(a)
        for (c, d), (q, k, v, gcum_r, gcum_c, beta_c), kbi, dc, t_inv in zip(chains, ld, kb, decay, inv):
            rows = _chunk_rows(c)
            eg_c = jnp.exp(gcum_c)
            rhs_hi, rhs_lo = _split(jnp.concatenate([v * beta_c, kbi * eg_c], axis=1))
            t_inv = t_inv.astype(bf16)
            uw = (jnp.dot(t_inv, rhs_hi, preferred_element_type=f32)
                  + jnp.dot(t_inv, rhs_lo, preferred_element_type=f32))
            u_s[d, rows, :] = uw[:, :HD]
            wq_s[d, _chunk_pair_rows(c), :] = jnp.concatenate([uw[:, HD:], q * eg_c], axis=0).astype(bf16)
            att = lax.dot_general(q.astype(bf16), k.astype(bf16), NT, preferred_element_type=f32) * dc
            at_s[d, rows, :] = att.astype(bf16)
            glast = gcum_r[:, 0:1] if d == 1 else gcum_r[:, CHUNK - 1:CHUNK]
            kd_s[d, rows, :] = (k * jnp.exp(glast - gcum_c)).astype(bf16)
            eg_s[d, _chunk_stat_rows(c), :] = jnp.broadcast_to(jnp.exp(glast), (8, HD))
        return carry

    lax.fori_loop(0, nc // group, prep_group, 0)


def _delta_body(xq_ref, xk_ref, xv_ref, xz_ref, cwq_ref, cwk_ref, cwv_ref, gt_ref, gf_ref, gb_ref,
                s0_ref, cn_ref, o_ref, sout_ref,
                q_s, k_s, v_s, u_s, wq_s, kd_s, at_s, eg_s, o_s, *, T, hps):
    nc = T // CHUNK
    for hh in range(hps):
        _delta_prepare(pl.program_id(1) * hps + hh, xq_ref.at[hh], xk_ref.at[hh], xv_ref.at[hh],
                       cwq_ref.at[hh], cwk_ref.at[hh], cwv_ref.at[hh], gt_ref,
                       (gf_ref.at[0, hh], gb_ref.at[0, hh]),
                       q_s.at[hh], k_s.at[hh], v_s.at[hh], u_s.at[hh], wq_s.at[hh], kd_s.at[hh], at_s.at[hh],
                       eg_s.at[hh], T)

    chains = [(hh, d) for hh in range(hps) for d in (0, 1)]

    def rec_step(accumulate):
        def step(i, carry):
            cs = [i if d == 0 else nc - 1 - i for hh, d in chains]
            r = [jnp.dot(wq_s[hh, d, _chunk_pair_rows(c), :], s.astype(bf16), preferred_element_type=f32)
                 for (hh, d), c, s in zip(chains, cs, carry)]
            vn = [(u_s[hh, d, _chunk_rows(c), :] - x[:CHUNK]).astype(bf16) for (hh, d), c, x in zip(chains, cs, r)]
            o = [x[CHUNK:] + jnp.dot(at_s[hh, d, _chunk_rows(c), :], y, preferred_element_type=f32)
                 for (hh, d), c, x, y in zip(chains, cs, r, vn)]
            upd = [lax.dot_general(kd_s[hh, d, _chunk_rows(c), :], y, TN_, preferred_element_type=f32)
                   for (hh, d), c, y in zip(chains, cs, vn)]
            for (hh, d), c, x in zip(chains, cs, o):
                if accumulate:
                    o_s[hh, _chunk_rows(c), :] += x
                else:
                    o_s[hh, _chunk_rows(c), :] = x
            return tuple(s * eg_s[hh, d, _chunk_stat_rows(c), :][0:1] + x
                         for (hh, d), c, s, x in zip(chains, cs, carry, upd))
        return step

    init = tuple(s0_ref[0, d, hh] for hh in range(hps) for d in (0, 1))
    mid = lax.fori_loop(0, nc // 2, rec_step(False), init)
    fin = lax.fori_loop(nc // 2, nc, rec_step(True), mid)
    for hh in range(hps):
        sout_ref[0, 0, hh] = fin[2 * hh]
        sout_ref[0, 1, hh] = fin[2 * hh + 1]
        o = o_s[hh]
        o = o * lax.rsqrt(jnp.mean(o * o, axis=-1, keepdims=True) + EPS) * cn_ref[...]
        z = xz_ref[hh]
        o_ref[:, hh * HD:(hh + 1) * HD] = (o * (z * _sigmoid(z))).astype(bf16)


def delta_mixer(p_rec, gates, gates_t, conv_w, c_norm, s0, T, row0, hps):
    B = s0.shape[0]
    nc = T // CHUNK
    rb0 = row0 // T

    def xspec(slab0):
        return pl.BlockSpec((hps, T, HD), lambda b, h: (slab0 // hps + h, rb0 + b, 0))

    def cwspec(slab0):
        return pl.BlockSpec((hps, 3, HD), lambda b, h: (slab0 // hps + h, 0, 0))

    def gspec(col0):
        return pl.BlockSpec((1, hps, nc, CHUNK), lambda b, h: (b, col0 // hps + h, 0, 0))

    sspec = pl.BlockSpec((1, 2, hps, HD, HD), lambda b, h: (b, 0, h, 0, 0))
    o, sout = pl.pallas_call(
        functools.partial(_delta_body, T=T, hps=hps),
        out_shape=(jax.ShapeDtypeStruct((B * T, HEADS * HD), bf16),
                   jax.ShapeDtypeStruct((B, 2, HEADS, HD, HD), f32)),
        grid=(B, HEADS // hps),
        in_specs=[xspec(0), xspec(8), xspec(16), xspec(24), cwspec(0), cwspec(8), cwspec(16),
                  pl.BlockSpec((T, LANES), lambda b, h: (rb0 + b, 0)),
                  gspec(16), gspec(24), sspec,
                  pl.BlockSpec((1, HD), lambda b, h: (0, 0))],
        out_specs=(pl.BlockSpec((T, hps * HD), lambda b, h: (b, h)), sspec),
        scratch_shapes=[pltpu.VMEM((hps, T, HD), f32), pltpu.VMEM((hps, T, HD), f32), pltpu.VMEM((hps, T, HD), f32),
                        pltpu.VMEM((hps, 2, T, HD), f32), pltpu.VMEM((hps, 2, 2 * T, HD), bf16),
                        pltpu.VMEM((hps, 2, T, HD), bf16), pltpu.VMEM((hps, 2, T, CHUNK), bf16),
                        pltpu.VMEM((hps, 2, nc * 8, HD), f32), pltpu.VMEM((hps, T, HD), f32)],
        compiler_params=_cparams(("arbitrary", "arbitrary")),
        name="delta_mixer",
    )(p_rec, p_rec, p_rec, p_rec, conv_w, conv_w, conv_w, gates, gates_t, gates_t, s0,
      c_norm.reshape(1, HD))
    return o, sout


def _mlstm_prepare(head, m0, xq_ref, xk_ref, gt_ref, i_rows, b_rows, qk_s, kw_s, col_s, ds_s, T):
    nc = T // CHUNK
    incl = (_tri(False, False), _tri(True, False))
    lane = lax.broadcasted_iota(jnp.int32, (CHUNK, HD), 1)
    row8 = lax.broadcasted_iota(jnp.int32, (8, HD), 0)
    group = min(MLSTM_PREP_CHUNKS, nc)

    def prep_group(t, carry):
        chains = [(t * group + i if d == 0 else nc - 1 - (t * group + i), d)
                  for i in range(group) for d in (0, 1)]
        ld = []
        for c, d in chains:
            rows = _chunk_rows(c)
            q = (xq_ref[rows, :] * ATTN_SCALE).astype(bf16)
            k = xk_ref[rows, :]
            i_r = i_rows[d][pl.ds(c, 1), :]
            b_r = b_rows[d][pl.ds(c, 1), :]
            i_c, b_c = _gate_cols(gt_ref, rows, (4 * HEADS + d * HEADS + head, 6 * HEADS + d * HEADS + head))
            dmat = jnp.where(incl[d], b_c - b_r + i_r, -jnp.inf)
            dmax = jnp.max(dmat, axis=-1, keepdims=True)
            qk = lax.dot_general(q, k.astype(bf16), NT, preferred_element_type=f32)
            ld.append((k, b_r, b_c, i_c, dmat, dmax, qk))
        m = list(carry)
        for (c, d), (k, b_r, b_c, i_c, dmat, dmax, qk) in zip(chains, ld):
            last = 0 if d == 1 else CHUNK - 1
            rows = _chunk_rows(c)
            inter = b_c + m[d]
            mt = jnp.maximum(inter, dmax)
            qkw = qk * jnp.exp(dmat - mt)
            m_new = mt[last:last + 1, :]
            b_last = b_r[:, last:last + 1]
            kw = k * jnp.exp(b_last - b_c + i_c - m_new)
            qk_s[d, rows, :] = qkw.astype(bf16)
            kw_s[d, rows, :] = kw.astype(bf16)
            col_s[d, rows, :] = jnp.where(lane == 0, jnp.exp(inter - mt),
                                          jnp.where(lane == 1, jnp.sum(qkw, axis=-1, keepdims=True), jnp.exp(-mt)))
            ds = jnp.broadcast_to(jnp.exp(b_last + m[d] - m_new), (8, HD))
            ksum = jnp.broadcast_to(jnp.sum(kw, axis=0, keepdims=True), (8, HD))
            ds_s[d, _chunk_stat_rows(c), :] = jnp.where(row8 == 0, ds, ksum)
            m[d] = m_new
        return tuple(m)

    return lax.fori_loop(0, nc // group, prep_group, m0)


def _mlstm_body(m0_ref, xq_ref, xk_ref, xv_ref, xo_ref, gt_ref, if_ref, ib_ref, ff_ref, fb_ref, c0_ref, n0_ref,
                dn_ref, o_ref, cout_ref, nout_ref, mout_ref,
                qk_s, kw_s, col_s, ds_s, h_s, *, T, hps):
    nc = T // CHUNK
    b_id = pl.program_id(0)
    for hh in range(hps):
        head = pl.program_id(1) * hps + hh
        m0 = (jnp.full((1, 1), m0_ref[b_id, 0, head], f32), jnp.full((1, 1), m0_ref[b_id, 1, head], f32))
        mf, mb = _mlstm_prepare(head, m0, xq_ref.at[hh], xk_ref.at[hh], gt_ref,
                                (if_ref.at[0, hh], ib_ref.at[0, hh]), (ff_ref.at[0, hh], fb_ref.at[0, hh]),
                                qk_s.at[hh], kw_s.at[hh], col_s.at[hh], ds_s.at[hh], T)
        mout_ref[0, 0, hh] = jnp.broadcast_to(mf, (1, HD))
        mout_ref[0, 1, hh] = jnp.broadcast_to(mb, (1, HD))

    chains = [(hh, d) for hh in range(hps) for d in (0, 1)]

    def rec_step(accumulate):
        def step(i, carry):
            cs = [i if d == 0 else nc - 1 - i for hh, d in chains]
            cms, ns = carry[0::2], carry[1::2]
            q = [xq_ref[hh, _chunk_rows(c), :] * ATTN_SCALE for (hh, d), c in zip(chains, cs)]
            v = [xv_ref[hh, _chunk_rows(c), :].astype(bf16) for (hh, d), c in zip(chains, cs)]
            qc = [jnp.dot(x.astype(bf16), cm.astype(bf16), preferred_element_type=f32) for x, cm in zip(q, cms)]
            qv = [jnp.dot(qk_s[hh, d, _chunk_rows(c), :], y, preferred_element_type=f32)
                  for (hh, d), c, y in zip(chains, cs, v)]
            upd = [lax.dot_general(kw_s[hh, d, _chunk_rows(c), :], y, TN_, preferred_element_type=f32)
                   for (hh, d), c, y in zip(chains, cs, v)]
            out = []
            for (hh, d), c, x, a, b, u, cm, n in zip(chains, cs, q, qc, qv, upd, cms, ns):
                rows = _chunk_rows(c)
                cols = col_s[hh, d, rows, :]
                w_inter = cols[:, 0:1]
                den = w_inter * jnp.sum(x * n, axis=-1, keepdims=True) + cols[:, 1:2]
                h = (w_inter * a + b) / jnp.maximum(jnp.abs(den), cols[:, 2:3])
                if accumulate:
                    h_s[hh, rows, :] += h
                else:
                    h_s[hh, rows, :] = h
                dsn = ds_s[hh, d, _chunk_stat_rows(c), :]
                out.extend((dsn[0:1] * cm + u, dsn[0:1] * n + dsn[1:2]))
            return tuple(out)
        return step

    init = []
    for hh in range(hps):
        for d in (0, 1):
            init.extend((c0_ref[0, d, hh], n0_ref[0, d, hh]))
    mid = lax.fori_loop(0, nc // 2, rec_step(False), tuple(init))
    fin = lax.fori_loop(nc // 2, nc, rec_step(True), mid)
    for hh in range(hps):
        for d in (0, 1):
            j = 2 * (2 * hh + d)
            cout_ref[0, d, hh] = fin[j]
            nout_ref[0, d, hh] = fin[j + 1]
        h = h_s[hh]
        h = h * lax.rsqrt(jnp.mean(h * h, axis=-1, keepdims=True) + EPS) * dn_ref[...]
        o_ref[:, hh * HD:(hh + 1) * HD] = (h * _sigmoid(xo_ref[hh])).astype(bf16)


def mlstm_mixer(p_rec, gates, gates_t, d_norm, c0, n0, m0, T, row0, hps):
    B = c0.shape[0]
    nc = T // CHUNK
    rb0 = row0 // T

    def xspec(slab0):
        return pl.BlockSpec((hps, T, HD), lambda b, h: (slab0 // hps + h, rb0 + b, 0))

    def gspec(col0):
        return pl.BlockSpec((1, hps, nc, CHUNK), lambda b, h: (b, col0 // hps + h, 0, 0))

    cspec = pl.BlockSpec((1, 2, hps, HD, HD), lambda b, h: (b, 0, h, 0, 0))
    vspec = pl.BlockSpec((1, 2, hps, 1, HD), lambda b, h: (b, 0, h, 0, 0))
    o, cout, nout, mout = pl.pallas_call(
        functools.partial(_mlstm_body, T=T, hps=hps),
        out_shape=(jax.ShapeDtypeStruct((B * T, HEADS * HD), bf16),
                   jax.ShapeDtypeStruct((B, 2, HEADS, HD, HD), f32),
                   jax.ShapeDtypeStruct((B, 2, HEADS, 1, HD), f32),
                   jax.ShapeDtypeStruct((B, 2, HEADS, 1, HD), f32)),
        grid=(B, HEADS // hps),
        in_specs=[pl.BlockSpec(memory_space=pltpu.SMEM),
                  xspec(32), xspec(40), xspec(48), xspec(56),
                  pl.BlockSpec((T, LANES), lambda b, h: (rb0 + b, 0)),
                  gspec(32), gspec(40), gspec(48), gspec(56), cspec, vspec,
                  pl.BlockSpec((1, HD), lambda b, h: (0, 0))],
        out_specs=(pl.BlockSpec((T, hps * HD), lambda b, h: (b, h)), cspec, vspec, vspec),
        scratch_shapes=[pltpu.VMEM((hps, 2, T, CHUNK), bf16), pltpu.VMEM((hps, 2, T, HD), bf16),
                        pltpu.VMEM((hps, 2, T, HD), f32), pltpu.VMEM((hps, 2, nc * 8, HD), f32),
                        pltpu.VMEM((hps, T, HD), f32)],
        compiler_params=_cparams(("arbitrary", "arbitrary")),
        name="mlstm_mixer",
    )(m0, p_rec, p_rec, p_rec, p_rec, gates, gates_t, gates_t, gates_t, gates_t, c0,
      n0.reshape(B, 2, HEADS, 1, HD), d_norm.reshape(1, HD))
    return o, cout, nout, mout


def _att_weight(w):
    return jnp.concatenate([w[:, :1024], w[:, 1536:], w[:, 1024:1536]], axis=1).astype(bf16)


def _rec_weight(w):
    pad = jnp.zeros((D_MODEL, REC_SLABS * LANES - 8256), w.dtype)
    return jnp.concatenate([w[:, :4096], w[:, 4128:8224], w[:, 4096:4128], w[:, 8224:], pad], axis=1).astype(bf16)


def _gates_rows(gates, row0, B, T):
    g = gates[row0:row0 + B * T, :64].reshape(B, T // CHUNK, CHUNK, 64)
    return jnp.transpose(g, (0, 3, 1, 2))


def kernel(x_prompt, x_sample, cache_a_k, cache_a_v, cache_b_k, cache_b_v, state_c, state_d_c, state_d_n, state_d_m, c, c_ctx, w_mod, b_mod, g_mix_pre, g_mix_post, g_ffn_pre, g_ffn_post, w_in_att, w_in_rec, w_out, a_sink, b_rel_bias, c_conv, c_a_log, c_dt_bias, c_norm, d_b_i, d_b_f, d_norm, w_up, w_down):
    x = jnp.concatenate([x_prompt.reshape(N_CTX, D_MODEL), x_sample.reshape(N_LAT, D_MODEL)], axis=0)
    c_all = jnp.concatenate([c_ctx[None, :], c, jnp.zeros((MOD_ROWS - 1 - DEC_BATCH, D_MODEL), f32)], axis=0)
    mods = adaln_all(c_all, w_mod, b_mod).reshape(DEPTH, MOD_ROWS, 6, D_MODEL)
    tables = _rope_tables()

    ka_c = cache_a_k.reshape(DEC_BATCH, N_ATT, PAST_LEN, A_KV_HEADS * HD)
    va_c = cache_a_v.reshape(DEC_BATCH, N_ATT, PAST_LEN, A_KV_HEADS * HD)
    kb_c = cache_b_k.reshape(DEC_BATCH, N_ATT, PAST_LEN, HEADS * HD)
    vb_c = cache_b_v.reshape(DEC_BATCH, N_ATT, PAST_LEN, HEADS * HD)

    att_kv = []
    rec_states = []
    for l in range(DEPTH):
        j = l // 2
        mod_l = mods[l]
        if l % 2 == 0:
            p, pb = norm_mod_project(x, g_mix_pre[l], mod_l, _att_weight(w_in_att[j]), q_slabs=2 * HEADS)
            att_kv.append(p[16:36, :N_CTX])
            pr = rope_lat(p, tables)
            sink = a_sink[j].astype(f32)
            bias_tab = _na_bias_tables(b_rel_bias[j].astype(f32))
            o_ctx = (ctx_attention_a(pb, sink), ctx_attention_b(pb))
            o_lat = (window_attention_lat(pr, pb, sink, ka_c, va_c, j),
                     neighbourhood_attention_lat(pb, bias_tab, kb_c, vb_c, j))
        else:
            p = norm_mod_project(x, g_mix_pre[l], mod_l, _rec_weight(w_in_rec[j]))
            gates = gate_activations(p, c_a_log[j], c_dt_bias[j], d_b_i[j], d_b_f[j])
            g_ctx = _gates_rows(gates, 0, BATCH, SEQ)
            g_lat = _gates_rows(gates, N_CTX, DEC_BATCH, DEC_SEQ)
            conv_w = jnp.transpose(c_conv[j].astype(f32).reshape(3, 3 * HEADS, HD), (1, 0, 2))
            zc = jnp.zeros((BATCH, 2, HEADS, HD, HD), f32)
            oc_ctx, sc_ctx = delta_mixer(p, gates, g_ctx, conv_w, c_norm[j].astype(f32), zc, SEQ, 0, HPS_CTX)
            oc_lat, _ = delta_mixer(p, gates, g_lat, conv_w, c_norm[j].astype(f32), state_c[:, j].astype(f32),
                                    DEC_SEQ, N_CTX, HPS_LAT)
            od_ctx, cd_ctx, nd_ctx, md_ctx = mlstm_mixer(
                p, gates, g_ctx, d_norm[j].astype(f32), zc, jnp.zeros((BATCH, 2, HEADS, HD), f32),
                jnp.zeros((BATCH, 2, HEADS), f32), SEQ, 0, HPS_CTX)
            od_lat, _, _, _ = mlstm_mixer(
                p, gates, g_lat, d_norm[j].astype(f32), state_d_c[:, j].astype(f32), state_d_n[:, j].astype(f32),
                state_d_m[:, j].astype(f32), DEC_SEQ, N_CTX, HPS_LAT)
            rec_states.append((sc_ctx, cd_ctx, nd_ctx.reshape(BATCH, 2, HEADS, HD), md_ctx[:, :, :, 0, 0]))
            o_ctx = (oc_ctx, od_ctx)
            o_lat = (oc_lat, od_lat)
        x = out_project_residual(o_ctx, o_lat, w_out[l].astype(bf16), x, g_mix_post[l], mod_l)
        x = mlp_residual(x, g_ffn_pre[l], g_ffn_post[l], mod_l, w_up[l].astype(bf16), w_down[l].astype(bf16))

    y_prompt = x[:N_CTX].reshape(BATCH, SEQ, D_MODEL)
    y_sample = x[N_CTX:].reshape(DEC_BATCH, DEC_SEQ, D_MODEL)

    def kv(slab0, n):
        per = [jnp.transpose(a[slab0:slab0 + n].reshape(n, BATCH, SEQ, HD), (1, 2, 0, 3)) for a in att_kv]
        return jnp.stack(per, axis=1)

    new_a_k, new_a_v = kv(16, A_KV_HEADS), kv(18, A_KV_HEADS)
    new_b_k, new_b_v = kv(0, HEADS), kv(8, HEADS)
    new_state_c = jnp.stack([s[0] for s in rec_states], axis=1)
    new_state_d_c = jnp.stack([s[1] for s in rec_states], axis=1)
    new_state_d_n = jnp.stack([s[2] for s in rec_states], axis=1)
    new_state_d_m = jnp.stack([s[3] for s in rec_states], axis=1)
    return (y_prompt, y_sample, new_a_k, new_a_v, new_b_k, new_b_v, new_state_c, new_state_d_c, new_state_d_n,
            new_state_d_m)
```

```python
import functools
import math

import numpy as np
import jax
import jax.numpy as jnp
from jax import lax
from jax.experimental import pallas as pl
from jax.experimental.pallas import tpu as pltpu

f32 = jnp.float32
bf16 = jnp.bfloat16
HI = lax.Precision.HIGHEST

D_MODEL = 2048
BATCH = 16
SEQ = 256
DEPTH = 4
DEC_BATCH = 8
DEC_SEQ = 2048
PAST_LEN = 512
GRID_W = 64
HD = 128
HEADS = 8
A_KV_HEADS = 2
A_REP = 4
FF = 4 * D_MODEL
WINDOW = 128
NA_ROWS = 8
NA_COLS = 16
CHUNK = 64
ROPE_BASE = 10000.0
EPS = 1e-6
N_ATT = 2
N_REC = 2
ATTN_SCALE = HD ** -0.5
GRID_ROWS = DEC_SEQ // GRID_W

N_CTX = BATCH * SEQ
N_LAT = DEC_BATCH * DEC_SEQ
N_ALL = N_CTX + N_LAT
MOD_ROWS = 16

ATT_SLABS = 36
REC_SLABS = 66
LANES = 128
NORM_ROWS = 64
NORM_UNROLL = 2
VMEM_LIMIT = 56 * 1024 * 1024

NT = (((1,), (1,)), ((), ()))
TN_ = (((0,), (0,)), ((), ()))


def _cparams(sem):
    return pltpu.CompilerParams(dimension_semantics=sem, vmem_limit_bytes=VMEM_LIMIT)


def _sigmoid(x):
    return 1.0 / (1.0 + jnp.exp(-x))


def _softplus(x):
    e = jnp.exp(-jnp.abs(x))
    u = 1.0 + e
    log1p = jnp.where(u == 1.0, e, jnp.log(u) * (e / (u - 1.0)))
    return jnp.maximum(x, 0.0) + log1p


def _mod_row(tm):
    ctx_tiles = N_CTX // tm
    per_batch = DEC_SEQ // tm
    return lambda i: jnp.where(i < ctx_tiles, 0, 1 + (i - ctx_tiles) // per_batch)


def _adaln_body(c_ref, w_ref, b_ref, o_ref):
    c = c_ref[...]
    s = (c * _sigmoid(c)).astype(bf16)
    o_ref[0] = jnp.dot(s, w_ref[0].astype(bf16), preferred_element_type=f32) + b_ref[0]


def adaln_all(c_all, w_mod, b_mod):
    tn = 1024
    return pl.pallas_call(
        _adaln_body,
        out_shape=jax.ShapeDtypeStruct((DEPTH, MOD_ROWS, 6 * D_MODEL), f32),
        grid=(DEPTH, 6 * D_MODEL // tn),
        in_specs=[pl.BlockSpec((MOD_ROWS, D_MODEL), lambda l, j: (0, 0)),
                  pl.BlockSpec((1, D_MODEL, tn), lambda l, j: (l, 0, j)),
                  pl.BlockSpec((1, 1, tn), lambda l, j: (l, 0, j))],
        out_specs=pl.BlockSpec((1, MOD_ROWS, tn), lambda l, j: (l, 0, j)),
        compiler_params=_cparams(("arbitrary", "arbitrary")),
        name="adaln",
    )(c_all, w_mod, b_mod.reshape(DEPTH, 1, 6 * D_MODEL))


def _norm_mod_rows(x_ref, g_ref, m_ref, h_ref, tm, sub, sh_row, sc_row):
    gain = g_ref[...] * (1.0 + m_ref[0, sc_row:sc_row + 1, :])
    sh = m_ref[0, sh_row:sh_row + 1, :]

    def body(r, carry):
        rows = pl.ds(pl.multiple_of(r * sub, sub), sub)
        x = x_ref[rows, :]
        ms = jnp.mean(x * x, axis=-1, keepdims=True)
        h_ref[rows, :] = (x * lax.rsqrt(ms + EPS) * gain + sh).astype(bf16)
        return carry

    lax.fori_loop(0, tm // sub, body, 0, unroll=NORM_UNROLL)


def _post_norm_residual_rows(y_ref, x_ref, g_ref, m_ref, tm, sub, gate_row):
    gain = g_ref[...] * m_ref[0, gate_row:gate_row + 1, :]

    def body(r, carry):
        rows = pl.ds(pl.multiple_of(r * sub, sub), sub)
        y = y_ref[rows, :]
        ms = jnp.mean(y * y, axis=-1, keepdims=True)
        y_ref[rows, :] = x_ref[rows, :] + y * lax.rsqrt(ms + EPS) * gain
        return carry

    lax.fori_loop(0, tm // sub, body, 0, unroll=NORM_UNROLL)


def _proj_body(x_ref, g_ref, m_ref, w_ref, o_ref, *rest, tm, tn, sub, q_slabs):
    h_ref = rest[-1]

    @pl.when(pl.program_id(1) == 0)
    def _():
        _norm_mod_rows(x_ref, g_ref, m_ref, h_ref, tm, sub, 0, 1)

    acc = jnp.dot(h_ref[...], w_ref[...], preferred_element_type=f32)
    for s in range(tn // LANES):
        o_ref[s] = acc[:, s * LANES:(s + 1) * LANES]
    if q_slabs:
        ob_ref = rest[0]
        slab0 = pl.program_id(1) * (tn // LANES)
        for s in range(tn // LANES):
            scale = jnp.where(slab0 + s < q_slabs, ATTN_SCALE, 1.0).astype(f32)
            ob_ref[s] = (acc[:, s * LANES:(s + 1) * LANES] * scale).astype(bf16)


def norm_mod_project(x, g_pre, mod_l, w_bf16, q_slabs=0):
    tm, tn, sub = 1024, 768, NORM_ROWS
    ncols = w_bf16.shape[1]
    mrow = _mod_row(tm)
    ospec = pl.BlockSpec((tn // LANES, tm, LANES), lambda i, j: (j, i, 0))
    oshape = jax.ShapeDtypeStruct((ncols // LANES, N_ALL, LANES), f32)
    return pl.pallas_call(
        functools.partial(_proj_body, tm=tm, tn=tn, sub=sub, q_slabs=q_slabs),
        out_shape=(oshape, jax.ShapeDtypeStruct(oshape.shape, bf16)) if q_slabs else oshape,
        grid=(N_ALL // tm, ncols // tn),
        in_specs=[pl.BlockSpec((tm, D_MODEL), lambda i, j: (i, 0)),
                  pl.BlockSpec((1, D_MODEL), lambda i, j: (0, 0)),
                  pl.BlockSpec((1, 6, D_MODEL), lambda i, j: (mrow(i), 0, 0)),
                  pl.BlockSpec((D_MODEL, tn), lambda i, j: (0, j))],
        out_specs=(ospec, ospec) if q_slabs else ospec,
        scratch_shapes=[pltpu.VMEM((tm, D_MODEL), bf16)],
        compiler_params=_cparams(("arbitrary", "arbitrary")),
        name="norm_mod_project",
    )(x, g_pre.reshape(1, D_MODEL), mod_l, w_bf16)


def _outproj_body(c1_ref, c2_ref, l1_ref, l2_ref, w_ref, x_ref, g_ref, m_ref, out_ref, *, tm, sub):
    half = D_MODEL // 2
    is_ctx = pl.program_id(0) < N_CTX // tm

    def project(o1_ref, o2_ref):
        out_ref[...] = (jnp.dot(o1_ref[...], w_ref[0:half, :], preferred_element_type=f32)
                        + jnp.dot(o2_ref[...], w_ref[half:D_MODEL, :], preferred_element_type=f32))

    @pl.when(is_ctx)
    def _():
        project(c1_ref, c2_ref)

    @pl.when(jnp.logical_not(is_ctx))
    def _():
        project(l1_ref, l2_ref)

    _post_norm_residual_rows(out_ref, x_ref, g_ref, m_ref, tm, sub, 2)


def out_project_residual(o_ctx, o_lat, w_bf16, x, g_post, mod_l):
    tm, sub = 512, NORM_ROWS
    half = D_MODEL // 2
    mrow = _mod_row(tm)
    ctx_tiles = N_CTX // tm
    cspec = pl.BlockSpec((tm, half), lambda i: (jnp.minimum(i, ctx_tiles - 1), 0))
    lspec = pl.BlockSpec((tm, half), lambda i: (jnp.maximum(i - ctx_tiles, 0), 0))
    return pl.pallas_call(
        functools.partial(_outproj_body, tm=tm, sub=sub),
        out_shape=jax.ShapeDtypeStruct((N_ALL, D_MODEL), f32),
        grid=(N_ALL // tm,),
        in_specs=[cspec, cspec, lspec, lspec,
                  pl.BlockSpec((D_MODEL, D_MODEL), lambda i: (0, 0)),
                  pl.BlockSpec((tm, D_MODEL), lambda i: (i, 0)),
                  pl.BlockSpec((1, D_MODEL), lambda i: (0, 0)),
                  pl.BlockSpec((1, 6, D_MODEL), lambda i: (mrow(i), 0, 0))],
        out_specs=pl.BlockSpec((tm, D_MODEL), lambda i: (i, 0)),
        compiler_params=_cparams(("arbitrary",)),
        name="out_project_residual",
    )(o_ctx[0], o_ctx[1], o_lat[0], o_lat[1], w_bf16, x, g_post.reshape(1, D_MODEL), mod_l)


def _mlp_body(x_ref, gpre_ref, gpost_ref, m_ref, wu_ref, wd_ref, out_ref, h_ref, *, tm, sub, nj):
    j = pl.program_id(1)

    @pl.when(j == 0)
    def _():
        _norm_mod_rows(x_ref, gpre_ref, m_ref, h_ref, tm, sub, 3, 4)
        out_ref[...] = jnp.zeros_like(out_ref)

    u = jnp.dot(h_ref[...], wu_ref[...], preferred_element_type=f32)
    u = jnp.maximum(u, 0.0)
    u = (u * u).astype(bf16)
    out_ref[...] = jnp.dot(u, wd_ref[...], preferred_element_type=f32) + out_ref[...]

    @pl.when(j == nj - 1)
    def _():
        _post_norm_residual_rows(out_ref, x_ref, gpost_ref, m_ref, tm, sub, 5)


def mlp_residual(x, g_pre, g_post, mod_l, wu_bf16, wd_bf16):
    tm, tf, sub = 512, 1024, NORM_ROWS
    nj = FF // tf
    mrow = _mod_row(tm)
    return pl.pallas_call(
        functools.partial(_mlp_body, tm=tm, sub=sub, nj=nj),
        out_shape=jax.ShapeDtypeStruct((N_ALL, D_MODEL), f32),
        grid=(N_ALL // tm, nj),
        in_specs=[pl.BlockSpec((tm, D_MODEL), lambda i, j: (i, 0)),
                  pl.BlockSpec((1, D_MODEL), lambda i, j: (0, 0)),
                  pl.BlockSpec((1, D_MODEL), lambda i, j: (0, 0)),
                  pl.BlockSpec((1, 6, D_MODEL), lambda i, j: (mrow(i), 0, 0)),
                  pl.BlockSpec((D_MODEL, tf), lambda i, j: (0, j)),
                  pl.BlockSpec((tf, D_MODEL), lambda i, j: (j, 0))],
        out_specs=pl.BlockSpec((tm, D_MODEL), lambda i, j: (i, 0)),
        scratch_shapes=[pltpu.VMEM((tm, D_MODEL), bf16)],
        compiler_params=_cparams(("arbitrary", "arbitrary")),
        name="mlp_residual",
    )(x, g_pre.reshape(1, D_MODEL), g_post.reshape(1, D_MODEL), mod_l, wu_bf16, wd_bf16)


def _rope_tables():
    nf = HD // 4
    t = jnp.arange(DEC_SEQ)
    row = (t // GRID_W).astype(f32)
    col = (t % GRID_W).astype(f32)
    inv = jnp.exp(-math.log(ROPE_BASE) * jnp.arange(nf, dtype=f32) / nf)
    lane = np.arange(HD)
    first_half = (lane % (2 * nf)) < nf
    pos = jnp.where(jnp.asarray(lane < HD // 2)[None, :], row[:, None], col[:, None])
    ang = pos * inv[jnp.asarray(lane % nf)][None, :]
    cos, sin = jnp.cos(ang), jnp.sin(ang)
    fh = jnp.asarray(first_half)[None, :]
    sin_a = jnp.where(fh, -sin, 0.0)
    sin_b = jnp.where(fh, 0.0, sin)

    half = (2, DEC_SEQ // 2, HD)
    return cos.reshape(half), sin_a.reshape(half), sin_b.reshape(half)


def _rope_body(p_ref, cos_ref, sa_ref, sb_ref, o_ref):
    t = pl.program_id(1) % 2
    scale = jnp.where(pl.program_id(0) < HEADS, ATTN_SCALE, 1.0).astype(f32)
    x = p_ref[0]
    y = x * cos_ref[t] + pltpu.roll(x, 96, 1) * sa_ref[t] + pltpu.roll(x, 32, 1) * sb_ref[t]
    o_ref[0] = (y * scale).astype(bf16)


def rope_lat(p_att, tables):
    tm = DEC_SEQ // 2
    ctx_tiles = N_CTX // tm
    cos, sa, sb = tables
    tspec = pl.BlockSpec((2, tm, HD), lambda s, i: (0, 0, 0))
    return pl.pallas_call(
        _rope_body,
        out_shape=jax.ShapeDtypeStruct((HEADS + A_KV_HEADS, N_LAT, HD), bf16),
        grid=(HEADS + A_KV_HEADS, N_LAT // tm),
        in_specs=[pl.BlockSpec((1, tm, HD), lambda s, i: (jnp.where(s < HEADS, s, s + 24), ctx_tiles + i, 0)),
                  tspec, tspec, tspec],
        out_specs=pl.BlockSpec((1, tm, HD), lambda s, i: (s, i, 0)),
        compiler_params=_cparams(("arbitrary", "arbitrary")),
        name="rope_lat",
    )(p_att, cos, sa, sb)


def _ctx_attn_body(*refs, reps, use_sink):
    if use_sink:
        sink_ref, q_ref, k_ref, v_ref, o_ref = refs
    else:
        q_ref, k_ref, v_ref, o_ref = refs
    g = pl.program_id(1)
    k = k_ref[0]
    v = v_ref[0]
    for r in range(reps):
        s = lax.dot_general(q_ref[r], k, NT, preferred_element_type=f32)
        m = jnp.max(s, axis=-1, keepdims=True)
        if use_sink:
            sk = sink_ref[g, r]
            m = jnp.maximum(m, sk)
        p = jnp.exp(s - m)
        den = jnp.sum(p, axis=-1, keepdims=True)
        if use_sink:
            den = den + jnp.exp(sk - m)
        o = jnp.dot(p.astype(bf16), v, preferred_element_type=f32) / den
        o_ref[:, r * HD:(r + 1) * HD] = o.astype(bf16)


def ctx_attention_a(pb, sink):
    return pl.pallas_call(
        functools.partial(_ctx_attn_body, reps=A_REP, use_sink=True),
        out_shape=jax.ShapeDtypeStruct((N_CTX, HEADS * HD), bf16),
        grid=(BATCH, A_KV_HEADS),
        in_specs=[pl.BlockSpec(memory_space=pltpu.SMEM),
                  pl.BlockSpec((A_REP, SEQ, HD), lambda b, g: (g, b, 0)),
                  pl.BlockSpec((1, SEQ, HD), lambda b, g: (32 + g, b, 0)),
                  pl.BlockSpec((1, SEQ, HD), lambda b, g: (34 + g, b, 0))],
        out_specs=pl.BlockSpec((SEQ, A_REP * HD), lambda b, g: (b, g)),
        compiler_params=_cparams(("arbitrary", "arbitrary")),
        name="ctx_attention_a",
    )(sink, pb, pb, pb)


def ctx_attention_b(pb):
    return pl.pallas_call(
        functools.partial(_ctx_attn_body, reps=1, use_sink=False),
        out_shape=jax.ShapeDtypeStruct((N_CTX, HEADS * HD), bf16),
        grid=(BATCH, HEADS),
        in_specs=[pl.BlockSpec((1, SEQ, HD), lambda b, h: (8 + h, b, 0)),
                  pl.BlockSpec((1, SEQ, HD), lambda b, h: (16 + h, b, 0)),
                  pl.BlockSpec((1, SEQ, HD), lambda b, h: (24 + h, b, 0))],
        out_specs=pl.BlockSpec((SEQ, HD), lambda b, h: (b, h)),
        compiler_params=_cparams(("arbitrary", "arbitrary")),
        name="ctx_attention_b",
    )(pb, pb, pb)


def _window_body(sink_ref, q_ref, k0_ref, k1_ref, k2_ref, v0_ref, v1_ref, v2_ref, kc_ref, vc_ref, o_ref, *, nb):
    g = pl.program_id(1)
    n = pl.program_id(2)
    rows = A_REP * WINDOW
    q = q_ref[...].reshape(rows, HD)
    kc = kc_ref[0, 0].astype(bf16)
    vc = vc_ref[0, 0].astype(bf16)
    s0 = lax.dot_general(q, k0_ref[0], NT, preferred_element_type=f32)
    s1 = lax.dot_general(q, k1_ref[0], NT, preferred_element_type=f32)
    s2 = lax.dot_general(q, k2_ref[0], NT, preferred_element_type=f32)
    sc = lax.dot_general(q, kc, NT, preferred_element_type=f32)
    qi = lax.broadcasted_iota(jnp.int32, (rows, WINDOW), 0) % WINDOW
    kj = lax.broadcasted_iota(jnp.int32, (rows, WINDOW), 1)
    neg = -jnp.inf
    s0 = jnp.where(kj >= qi, s0, neg)
    s0 = jnp.where(n > 0, s0, neg)
    s2 = jnp.where(kj <= qi, s2, neg)
    s2 = jnp.where(n < nb - 1, s2, neg)
    rep = lax.broadcasted_iota(jnp.int32, (rows, 1), 0) // WINDOW
    sk = jnp.zeros((rows, 1), f32)
    for r in range(A_REP):
        sk = jnp.where(rep == r, sink_ref[g, r], sk)
    m = jnp.maximum(jnp.maximum(jnp.max(s0, axis=-1, keepdims=True), jnp.max(s1, axis=-1, keepdims=True)),
                    jnp.maximum(jnp.max(s2, axis=-1, keepdims=True), jnp.max(sc, axis=-1, keepdims=True)))
    m = jnp.maximum(m, sk)
    p0 = jnp.exp(s0 - m)
    p1 = jnp.exp(s1 - m)
    p2 = jnp.exp(s2 - m)
    pc = jnp.exp(sc - m)
    den = (jnp.sum(p0, axis=-1, keepdims=True) + jnp.sum(p1, axis=-1, keepdims=True)
           + jnp.sum(p2, axis=-1, keepdims=True) + jnp.sum(pc, axis=-1, keepdims=True) + jnp.exp(sk - m))
    o = (jnp.dot(p0.astype(bf16), v0_ref[0], preferred_element_type=f32)
         + jnp.dot(p1.astype(bf16), v1_ref[0], preferred_element_type=f32)
         + jnp.dot(p2.astype(bf16), v2_ref[0], preferred_element_type=f32)
         + jnp.dot(pc.astype(bf16), vc, preferred_element_type=f32)) / den
    for r in range(A_REP):
        o_ref[:, r * HD:(r + 1) * HD] = o[r * WINDOW:(r + 1) * WINDOW].astype(bf16)


def window_attention_lat(pr, pb, sink, ka_c, va_c, j):
    nb = DEC_SEQ // WINDOW

    def qrow(b, n):
        return b * nb + n

    def kspec(slab0, shift, base=0):
        return pl.BlockSpec((1, WINDOW, HD),
                            lambda b, g, n: (slab0 + g, base + qrow(b, jnp.clip(n + shift, 0, nb - 1)), 0))

    vbase = N_CTX // WINDOW

    cspec = pl.BlockSpec((1, 1, PAST_LEN, HD), lambda b, g, n: (b, j, 0, g))
    return pl.pallas_call(
        functools.partial(_window_body, nb=nb),
        out_shape=jax.ShapeDtypeStruct((N_LAT, HEADS * HD), bf16),
        grid=(DEC_BATCH, A_KV_HEADS, nb),
        in_specs=[pl.BlockSpec(memory_space=pltpu.SMEM),
                  pl.BlockSpec((A_REP, WINDOW, HD), lambda b, g, n: (g, qrow(b, n), 0)),
                  kspec(HEADS, -1), kspec(HEADS, 0), kspec(HEADS, 1),
                  kspec(34, -1, vbase), kspec(34, 0, vbase), kspec(34, 1, vbase),
                  cspec, cspec],
        out_specs=pl.BlockSpec((WINDOW, A_REP * HD), lambda b, g, n: (b * nb + n, g)),
        compiler_params=_cparams(("arbitrary", "arbitrary", "arbitrary")),
        name="window_attention",
    )(sink, pr, pr, pr, pr, pb, pb, pb, ka_c, va_c)


NA_QROWS = 2
NA_WROWS = NA_QROWS + NA_ROWS - 1
NA_WIN_MAX = GRID_ROWS - NA_WROWS
NA_OFFS = (0, -2, -4, -5, -7)
NA_HEAD_GROUP = 4
NA_Q = NA_QROWS * GRID_W
NA_K = NA_WROWS * GRID_W


def _na_bias_tables(rel_bias):
    r0_of = {0: 0, -2: 2, -4: 4, -5: 28, -7: 30}
    cq = np.arange(GRID_W)[:, None]
    ck = np.arange(GRID_W)[None, :]
    wstart = np.clip(cq - NA_COLS // 2, 0, GRID_W - NA_COLS)
    col_ok = (ck >= wstart) & (ck < wstart + NA_COLS)
    dc = np.clip(ck - cq + NA_COLS - 1, 0, 2 * NA_COLS - 2)
    pick = ((dc[None] == np.arange(2 * NA_COLS - 1)[:, None, None]) & col_ok[None]).astype(np.float32)
    toep = jnp.einsum("hrd,dqk->hrqk", rel_bias, jnp.asarray(pick), precision=HI)
    toep = jnp.where(jnp.asarray(col_ok)[None, None], toep, -1e30)
    masked = jnp.full((HEADS, GRID_W, GRID_W), -1e30, f32)
    tables = []
    for off in NA_OFFS:
        r0 = r0_of[off]
        ws = int(np.clip(r0 - NA_ROWS // 2, 0, NA_WIN_MAX))
        assert ws - r0 == off
        qrows = []
        for a in range(NA_QROWS):
            r = r0 + a
            rs = int(np.clip(r - NA_ROWS // 2, 0, GRID_ROWS - NA_ROWS))
            blocks = []
            for w in range(NA_WROWS):
                krow = ws + w
                blocks.append(toep[:, krow - r + NA_ROWS - 1] if rs <= krow < rs + NA_ROWS else masked)
            qrows.append(jnp.concatenate(blocks, axis=2))
        tables.append(jnp.concatenate(qrows, axis=1))
    return jnp.stack(tables).astype(f32)


def _na_body(bias_ref, q_ref, k_ref, v_ref, kc_ref, vc_ref, o_ref, kcb_ref, vcb_ref):
    rb = pl.program_id(1)

    @pl.when(rb == 0)
    def _():
        kcb_ref[...] = kc_ref[0, 0].astype(bf16)
        vcb_ref[...] = vc_ref[0, 0].astype(bf16)

    ws = jnp.clip(NA_QROWS * rb - NA_ROWS // 2, 0, NA_WIN_MAX)
    win = pl.ds(pl.multiple_of(ws * GRID_W, GRID_W), NA_K)
    def lanes(h):
        return slice(h * HD, (h + 1) * HD)

    for h0 in range(0, HEADS, NA_HEAD_GROUP):
        hs = range(h0, h0 + NA_HEAD_GROUP)
        sl = [lax.dot_general(q_ref[h], k_ref[h, win, :], NT, preferred_element_type=f32) + bias_ref[0, h] for h in hs]
        sc = [lax.dot_general(q_ref[h], kcb_ref[:, lanes(h)], NT, preferred_element_type=f32) for h in hs]
        m = [jnp.maximum(jnp.max(a, axis=-1, keepdims=True), jnp.max(b, axis=-1, keepdims=True))
             for a, b in zip(sl, sc)]
        p_l = [jnp.exp(a - x) for a, x in zip(sl, m)]
        p_c = [jnp.exp(b - x) for b, x in zip(sc, m)]
        den = [jnp.sum(a, axis=-1, keepdims=True) + jnp.sum(b, axis=-1, keepdims=True) for a, b in zip(p_l, p_c)]
        o = [jnp.dot(a.astype(bf16), v_ref[h, win, :], preferred_element_type=f32)
             + jnp.dot(b.astype(bf16), vcb_ref[:, lanes(h)], preferred_element_type=f32)
             for h, a, b in zip(hs, p_l, p_c)]
        for h, x, d in zip(hs, o, den):
            o_ref[:, lanes(h)] = (x / d).astype(bf16)


def neighbourhood_attention_lat(pb, bias_tab, kb_c, vb_c, j):
    nrb = GRID_ROWS // NA_QROWS
    base = N_CTX // NA_Q
    per_b = DEC_SEQ // NA_Q
    seq_base = N_CTX // DEC_SEQ

    def tbl(rb):
        return jnp.where(rb < 2, rb, jnp.where(rb < nrb - 2, 2, rb - (nrb - 5)))

    cspec = pl.BlockSpec((1, 1, PAST_LEN, HEADS * HD), lambda b, rb: (b, j, 0, 0))
    return pl.pallas_call(
        _na_body,
        out_shape=jax.ShapeDtypeStruct((N_LAT, HEADS * HD), bf16),
        grid=(DEC_BATCH, nrb),
        in_specs=[pl.BlockSpec((1, HEADS, NA_Q, NA_K), lambda b, rb: (tbl(rb), 0, 0, 0)),
                  pl.BlockSpec((HEADS, NA_Q, HD), lambda b, rb: (1, base + b * per_b + rb, 0)),
                  pl.BlockSpec((HEADS, DEC_SEQ, HD), lambda b, rb: (2, seq_base + b, 0)),
                  pl.BlockSpec((HEADS, DEC_SEQ, HD), lambda b, rb: (3, seq_base + b, 0)),
                  cspec, cspec],
        out_specs=pl.BlockSpec((NA_Q, HEADS * HD), lambda b, rb: (b * per_b + rb, 0)),
        scratch_shapes=[pltpu.VMEM((PAST_LEN, HEADS * HD), bf16), pltpu.VMEM((PAST_LEN, HEADS * HD), bf16)],
        compiler_params=_cparams(("arbitrary", "arbitrary")),
        name="neighbourhood_attention",
    )(bias_tab, pb, pb, pb, kb_c, vb_c)


def _gates_body(p_ref, par_ref, o_ref, *, tm, blk):
    lane = lax.broadcasted_iota(jnp.int32, (blk, LANES), 1)
    ii = lax.broadcasted_iota(jnp.int32, (blk, blk), 0)
    jj = lax.broadcasted_iota(jnp.int32, (blk, blk), 1)
    same = (ii // CHUNK) == (jj // CHUNK)
    lower = jnp.where(same & (jj <= ii), 1.0, 0.0).astype(f32)
    upper = jnp.where(same & (jj >= ii), 1.0, 0.0).astype(f32)
    cumulated = ((lane >= 16) & (lane < 32)) | ((lane >= 48) & (lane < 64))
    backward = (lane % 16) >= 8
    for r in range(tm // blk):
        x = p_ref[0, r * blk:(r + 1) * blk, :]
        z = x + par_ref[1:2, :]
        act = jnp.where(lane < 16, _sigmoid(z),
                        jnp.where(lane < 32, par_ref[0:1, :] * _softplus(z), jnp.where(lane < 48, z, -_softplus(-z))))
        pre = jnp.dot(lower, act, precision=HI, preferred_element_type=f32)
        suf = jnp.dot(upper, act, precision=HI, preferred_element_type=f32)
        o_ref[r * blk:(r + 1) * blk, :] = jnp.where(cumulated, jnp.where(backward, suf, pre), act)


def gate_activations(p_rec, a_log, dt_bias, b_i, b_f):
    tm = 1024
    zeros16 = jnp.zeros((16,), f32)
    neg_a = jnp.concatenate([zeros16, -jnp.exp(a_log.reshape(16).astype(f32)), zeros16, zeros16, jnp.zeros((64,), f32)])
    bias = jnp.concatenate([zeros16, dt_bias.reshape(16).astype(f32), b_i.reshape(16).astype(f32),
                            b_f.reshape(16).astype(f32), jnp.zeros((64,), f32)])
    par = jnp.concatenate([neg_a[None], bias[None], jnp.zeros((6, LANES), f32)], axis=0)
    return pl.pallas_call(
        functools.partial(_gates_body, tm=tm, blk=4 * CHUNK),
        out_shape=jax.ShapeDtypeStruct((N_ALL, LANES), f32),
        grid=(N_ALL // tm,),
        in_specs=[pl.BlockSpec((1, tm, LANES), lambda i: (64, i, 0)),
                  pl.BlockSpec((8, LANES), lambda i: (0, 0))],
        out_specs=pl.BlockSpec((tm, LANES), lambda i: (i, 0)),
        compiler_params=_cparams(("arbitrary",)),
        name="gate_activations",
    )(p_rec, par)


def _tri(reverse, strict):
    ii = lax.broadcasted_iota(jnp.int32, (CHUNK, CHUNK), 0)
    jj = lax.broadcasted_iota(jnp.int32, (CHUNK, CHUNK), 1)
    if reverse:
        return (jj > ii) if strict else (jj >= ii)
    return (jj < ii) if strict else (jj <= ii)


def _split(x):
    hi = x.astype(bf16)
    return hi, (x - hi.astype(f32)).astype(bf16)


def _mm(a, b):
    return jnp.dot(a.astype(bf16), b.astype(bf16), preferred_element_type=f32)


def _unit_tri_inverses(mats):
    ii = lax.broadcasted_iota(jnp.int32, (CHUNK, CHUNK), 0)
    jj = lax.broadcasted_iota(jnp.int32, (CHUNK, CHUNK), 1)
    same16 = (ii // 16) == (jj // 16)
    same32 = (ii // 32) == (jj // 32)
    eye = jnp.where(ii == jj, 1.0, 0.0).astype(f32)
    m = [jnp.where(same16, -a, 0.0) for a in mats]
    l1 = [jnp.where(same16, 0.0, jnp.where(same32, a, 0.0)) for a in mats]
    l2 = [jnp.where(same32, 0.0, a) for a in mats]
    p = [eye + x for x in m]
    m = [_mm(x, x) for x in m]
    for _ in range(2):
        r = [_mm(jnp.concatenate([x, y], axis=0), x) for x, y in zip(m, p)]
        m = [x[:CHUNK] for x in r]
        p = [y + x[CHUNK:] for x, y in zip(r, p)]
    p = [y + _mm(y, x) for x, y in zip(m, p)]
    p = [y - _mm(y, _mm(l, y)) for l, y in zip(l1, p)]
    return [y - _mm(y, _mm(l, y)) for l, y in zip(l2, p)]


def _gate_cols(gt_ref, rows, lane_ids):
    x = gt_ref[rows, :]
    lane = lax.broadcasted_iota(jnp.int32, x.shape, 1)
    return [jnp.sum(jnp.where(lane == i, x, 0.0), axis=-1, keepdims=True) for i in lane_ids]


DELTA_PREP_CHUNKS = 8
MLSTM_PREP_CHUNKS = 4
HPS_CTX = 4
HPS_LAT = 2


def _chunk_rows(c):
    return pl.ds(pl.multiple_of(c * CHUNK, CHUNK), CHUNK)


def _chunk_pair_rows(c):
    return pl.ds(pl.multiple_of(c * 2 * CHUNK, 2 * CHUNK), 2 * CHUNK)


def _chunk_stat_rows(c):
    return pl.ds(pl.multiple_of(c * 8, 8), 8)


def _conv_silu(x_ref, cw_ref, T):
    x = x_ref[...]
    w = cw_ref[...]
    row = lax.broadcasted_iota(jnp.int32, (T, HD), 0)
    xp = jnp.where(row == 0, 0.0, pltpu.roll(x, 1, 0))
    xn = jnp.where(row == T - 1, 0.0, pltpu.roll(x, T - 1, 0))
    y = w[0:1] * xp + w[1:2] * x + w[2:3] * xn
    return y * _sigmoid(y)


def _delta_prepare(head, xq_ref, xk_ref, xv_ref, cwq_ref, cwk_ref, cwv_ref, gt_ref, gcum_rows,
                   q_s, k_s, v_s, u_s, wq_s, kd_s, at_s, eg_s, T):
    nc = T // CHUNK
    q = _conv_silu(xq_ref, cwq_ref, T)
    q_s[...] = q * lax.rsqrt(jnp.sum(q * q, axis=-1, keepdims=True) + EPS) * ATTN_SCALE
    k = _conv_silu(xk_ref, cwk_ref, T)
    k_s[...] = k * lax.rsqrt(jnp.sum(k * k, axis=-1, keepdims=True) + EPS)
    v_s[...] = _conv_silu(xv_ref, cwv_ref, T)
    incl = (_tri(False, False), _tri(True, False))
    strict = (_tri(False, True), _tri(True, True))
    group = min(DELTA_PREP_CHUNKS, nc)

    def prep_group(t, carry):
        chains = [(t * group + i, d) for i in range(group) for d in (0, 1)]
        ld = []
        for c, d in chains:
            rows = _chunk_rows(c)
            beta_c, gcum_c = _gate_cols(gt_ref, rows, (d * HEADS + head, 2 * HEADS + d * HEADS + head))
            ld.append((q_s[rows, :], k_s[rows, :], v_s[rows, :], gcum_rows[d][pl.ds(c, 1), :], gcum_c, beta_c))
        decay = [jnp.exp(jnp.where(incl[d], x[4] - x[3], -jnp.inf)) for (c, d), x in zip(chains, ld)]
        kb = [x[1] * x[5] for x in ld]
        a = [jnp.where(strict[d], lax.dot_general(y.astype(bf16), x[1].astype(bf16), NT,
                                                  preferred_element_type=f32) * dc, 0.0)
             for (c, d), x, y, dc in zip(chains, ld, kb, decay)]
        inv = _unit_tri_inverses(a)
        for (c, d), (q, k, v, gcum_r, gcum_c, beta_c), kbi, dc, t_inv in zip(chains, ld, kb, decay, inv):
            rows = _chunk_rows(c)
            eg_c = jnp.exp(gcum_c)
            rhs_hi, rhs_lo = _split(jnp.concatenate([v * beta_c, kbi * eg_c], axis=1))
            t_inv = t_inv.astype(bf16)
            uw = (jnp.dot(t_inv, rhs_hi, preferred_element_type=f32)
                  + jnp.dot(t_inv, rhs_lo, preferred_element_type=f32))
            u_s[d, rows, :] = uw[:, :HD]
            wq_s[d, _chunk_pair_rows(c), :] = jnp.concatenate([uw[:, HD:], q * eg_c], axis=0).astype(bf16)
            att = lax.dot_general(q.astype(bf16), k.astype(bf16), NT, preferred_element_type=f32) * dc
            at_s[d, rows, :] = att.astype(bf16)
            glast = gcum_r[:, 0:1] if d == 1 else gcum_r[:, CHUNK - 1:CHUNK]
            kd_s[d, rows, :] = (k * jnp.exp(glast - gcum_c)).astype(bf16)
            eg_s[d, _chunk_stat_rows(c), :] = jnp.broadcast_to(jnp.exp(glast), (8, HD))
        return carry

    lax.fori_loop(0, nc // group, prep_group, 0)


def _delta_body(xq_ref, xk_ref, xv_ref, xz_ref, cwq_ref, cwk_ref, cwv_ref, gt_ref, gf_ref, gb_ref,
                s0_ref, cn_ref, o_ref, sout_ref,
                q_s, k_s, v_s, u_s, wq_s, kd_s, at_s, eg_s, o_s, *, T, hps):
    nc = T // CHUNK
    for hh in range(hps):
        _delta_prepare(pl.program_id(1) * hps + hh, xq_ref.at[hh], xk_ref.at[hh], xv_ref.at[hh],
                       cwq_ref.at[hh], cwk_ref.at[hh], cwv_ref.at[hh], gt_ref,
                       (gf_ref.at[0, hh], gb_ref.at[0, hh]),
                       q_s.at[hh], k_s.at[hh], v_s.at[hh], u_s.at[hh], wq_s.at[hh], kd_s.at[hh], at_s.at[hh],
                       eg_s.at[hh], T)

    chains = [(hh, d) for hh in range(hps) for d in (0, 1)]

    def rec_step(accumulate):
        def step(i, carry):
            cs = [i if d == 0 else nc - 1 - i for hh, d in chains]
            r = [jnp.dot(wq_s[hh, d, _chunk_pair_rows(c), :], s.astype(bf16), preferred_element_type=f32)
                 for (hh, d), c, s in zip(chains, cs, carry)]
            vn = [(u_s[hh, d, _chunk_rows(c), :] - x[:CHUNK]).astype(bf16) for (hh, d), c, x in zip(chains, cs, r)]
            o = [x[CHUNK:] + jnp.dot(at_s[hh, d, _chunk_rows(c), :], y, preferred_element_type=f32)
                 for (hh, d), c, x, y in zip(chains, cs, r, vn)]
            upd = [lax.dot_general(kd_s[hh, d, _chunk_rows(c), :], y, TN_, preferred_element_type=f32)
                   for (hh, d), c, y in zip(chains, cs, vn)]
            for (hh, d), c, x in zip(chains, cs, o):
                if accumulate:
                    o_s[hh, _chunk_rows(c), :] += x
                else:
                    o_s[hh, _chunk_rows(c), :] = x
            return tuple(s * eg_s[hh, d, _chunk_stat_rows(c), :][0:1] + x
                         for (hh, d), c, s, x in zip(chains, cs, carry, upd))
        return step

    init = tuple(s0_ref[0, d, hh] for hh in range(hps) for d in (0, 1))
    mid = lax.fori_loop(0, nc // 2, rec_step(False), init)
    fin = lax.fori_loop(nc // 2, nc, rec_step(True), mid)
    for hh in range(hps):
        sout_ref[0, 0, hh] = fin[2 * hh]
        sout_ref[0, 1, hh] = fin[2 * hh + 1]
        o = o_s[hh]
        o = o * lax.rsqrt(jnp.mean(o * o, axis=-1, keepdims=True) + EPS) * cn_ref[...]
        z = xz_ref[hh]
        o_ref[:, hh * HD:(hh + 1) * HD] = (o * (z * _sigmoid(z))).astype(bf16)


def delta_mixer(p_rec, gates, gates_t, conv_w, c_norm, s0, T, row0, hps):
    B = s0.shape[0]
    nc = T // CHUNK
    rb0 = row0 // T

    def xspec(slab0):
        return pl.BlockSpec((hps, T, HD), lambda b, h: (slab0 // hps + h, rb0 + b, 0))

    def cwspec(slab0):
        return pl.BlockSpec((hps, 3, HD), lambda b, h: (slab0 // hps + h, 0, 0))

    def gspec(col0):
        return pl.BlockSpec((1, hps, nc, CHUNK), lambda b, h: (b, col0 // hps + h, 0, 0))

    sspec = pl.BlockSpec((1, 2, hps, HD, HD), lambda b, h: (b, 0, h, 0, 0))
    o, sout = pl.pallas_call(
        functools.partial(_delta_body, T=T, hps=hps),
        out_shape=(jax.ShapeDtypeStruct((B * T, HEADS * HD), bf16),
                   jax.ShapeDtypeStruct((B, 2, HEADS, HD, HD), f32)),
        grid=(B, HEADS // hps),
        in_specs=[xspec(0), xspec(8), xspec(16), xspec(24), cwspec(0), cwspec(8), cwspec(16),
                  pl.BlockSpec((T, LANES), lambda b, h: (rb0 + b, 0)),
                  gspec(16), gspec(24), sspec,
                  pl.BlockSpec((1, HD), lambda b, h: (0, 0))],
        out_specs=(pl.BlockSpec((T, hps * HD), lambda b, h: (b, h)), sspec),
        scratch_shapes=[pltpu.VMEM((hps, T, HD), f32), pltpu.VMEM((hps, T, HD), f32), pltpu.VMEM((hps, T, HD), f32),
                        pltpu.VMEM((hps, 2, T, HD), f32), pltpu.VMEM((hps, 2, 2 * T, HD), bf16),
                        pltpu.VMEM((hps, 2, T, HD), bf16), pltpu.VMEM((hps, 2, T, CHUNK), bf16),
                        pltpu.VMEM((hps, 2, nc * 8, HD), f32), pltpu.VMEM((hps, T, HD), f32)],
        compiler_params=_cparams(("arbitrary", "arbitrary")),
        name="delta_mixer",
    )(p_rec, p_rec, p_rec, p_rec, conv_w, conv_w, conv_w, gates, gates_t, gates_t, s0,
      c_norm.reshape(1, HD))
    return o, sout


def _mlstm_prepare(head, m0, xq_ref, xk_ref, gt_ref, i_rows, b_rows, qk_s, kw_s, col_s, ds_s, T):
    nc = T // CHUNK
    incl = (_tri(False, False), _tri(True, False))
    lane = lax.broadcasted_iota(jnp.int32, (CHUNK, HD), 1)
    row8 = lax.broadcasted_iota(jnp.int32, (8, HD), 0)
    eye = jnp.where(lax.broadcasted_iota(jnp.int32, (HD, HD), 0) == lax.broadcasted_iota(jnp.int32, (HD, HD), 1),
                    1.0, 0.0).astype(bf16)
    group = min(MLSTM_PREP_CHUNKS, nc)

    def prep_group(t, carry):
        chains = [(t * group + i if d == 0 else nc - 1 - (t * group + i), d)
                  for i in range(group) for d in (0, 1)]
        ld = []
        for c, d in chains:
            rows = _chunk_rows(c)
            q = (xq_ref[rows, :] * ATTN_SCALE).astype(bf16)
            k = xk_ref[rows, :]
            i_r = i_rows[d][pl.ds(c, 1), :]
            b_r = b_rows[d][pl.ds(c, 1), :]
            i_c, b_c = _gate_cols(gt_ref, rows, (4 * HEADS + d * HEADS + head, 6 * HEADS + d * HEADS + head))
            dmat = jnp.where(incl[d], b_c - b_r + i_r, -jnp.inf)
            dmax = jnp.max(dmat, axis=-1, keepdims=True)
            qk = lax.dot_general(q, k.astype(bf16), NT, preferred_element_type=f32)
            ld.append((k, b_r, b_c, i_c, dmat, dmax, qk))
        m = list(carry)
        for (c, d), (k, b_r, b_c, i_c, dmat, dmax, qk) in zip(chains, ld):
            last = 0 if d == 1 else CHUNK - 1
            rows = _chunk_rows(c)
            inter = b_c + m[d]
            mt = jnp.maximum(inter, dmax)
            qkw = qk * jnp.exp(dmat - mt)
            m_new = mt[last:last + 1, :]
            b_last = b_r[:, last:last + 1]
            kw = k * jnp.exp(b_last - b_c + i_c - m_new)
            qk_s[d, rows, :] = qkw.astype(bf16)
            kw_s[d, _chunk_pair_rows(c), :] = lax.dot_general(eye, kw.astype(bf16), NT,
                                                              preferred_element_type=f32).astype(bf16)
            col_s[d, rows, :] = jnp.where(lane == 0, jnp.exp(inter - mt),
                                          jnp.where(lane == 1, jnp.sum(qkw, axis=-1, keepdims=True), jnp.exp(-mt)))
            ds = jnp.broadcast_to(jnp.exp(b_last + m[d] - m_new), (8, HD))
            ksum = jnp.broadcast_to(jnp.sum(kw, axis=0, keepdims=True), (8, HD))
            ds_s[d, _chunk_stat_rows(c), :] = jnp.where(row8 == 0, ds, ksum)
            m[d] = m_new
        return tuple(m)

    return lax.fori_loop(0, nc // group, prep_group, m0)


def _mlstm_body(m0_ref, xq_ref, xk_ref, xv_ref, xo_ref, gt_ref, if_ref, ib_ref, ff_ref, fb_ref, c0_ref, n0_ref,
                dn_ref, o_ref, cout_ref, nout_ref, mout_ref,
                qk_s, kw_s, col_s, ds_s, h_s, *, T, hps):
    nc = T // CHUNK
    b_id = pl.program_id(0)
    for hh in range(hps):
        head = pl.program_id(1) * hps + hh
        m0 = (jnp.full((1, 1), m0_ref[b_id, 0, head], f32), jnp.full((1, 1), m0_ref[b_id, 1, head], f32))
        mf, mb = _mlstm_prepare(head, m0, xq_ref.at[hh], xk_ref.at[hh], gt_ref,
                                (if_ref.at[0, hh], ib_ref.at[0, hh]), (ff_ref.at[0, hh], fb_ref.at[0, hh]),
                                qk_s.at[hh], kw_s.at[hh], col_s.at[hh], ds_s.at[hh], T)
        mout_ref[0, 0, hh] = jnp.broadcast_to(mf, (1, HD))
        mout_ref[0, 1, hh] = jnp.broadcast_to(mb, (1, HD))

    chains = [(hh, d) for hh in range(hps) for d in (0, 1)]

    def rec_step(accumulate):
        def step(i, carry):
            cs = [i if d == 0 else nc - 1 - i for hh, d in chains]
            cms, ns = carry[0::2], carry[1::2]
            q = [xq_ref[hh, _chunk_rows(c), :] * ATTN_SCALE for (hh, d), c in zip(chains, cs)]
            v = [xv_ref[hh, _chunk_rows(c), :].astype(bf16) for (hh, d), c in zip(chains, cs)]
            qc = [jnp.dot(x.astype(bf16), cm.astype(bf16), preferred_element_type=f32) for x, cm in zip(q, cms)]
            qv = [jnp.dot(qk_s[hh, d, _chunk_rows(c), :], y, preferred_element_type=f32)
                  for (hh, d), c, y in zip(chains, cs, v)]
            upd = [jnp.dot(kw_s[hh, d, _chunk_pair_rows(c), :], y, preferred_element_type=f32)
                   for (hh, d), c, y in zip(chains, cs, v)]
            out = []
            for (hh, d), c, x, a, b, u, cm, n in zip(chains, cs, q, qc, qv, upd, cms, ns):
                rows = _chunk_rows(c)
                cols = col_s[hh, d, rows, :]
                w_inter = cols[:, 0:1]
                den = w_inter * jnp.sum(x * n, axis=-1, keepdims=True) + cols[:, 1:2]
                h = (w_inter * a + b) / jnp.maximum(jnp.abs(den), cols[:, 2:3])
                if accumulate:
                    h_s[hh, rows, :] += h
                else:
                    h_s[hh, rows, :] = h
                dsn = ds_s[hh, d, _chunk_stat_rows(c), :]
                out.extend((dsn[0:1] * cm + u, dsn[0:1] * n + dsn[1:2]))
            return tuple(out)
        return step

    init = []
    for hh in range(hps):
        for d in (0, 1):
            init.extend((c0_ref[0, d, hh], n0_ref[0, d, hh]))
    mid = lax.fori_loop(0, nc // 2, rec_step(False), tuple(init))
    fin = lax.fori_loop(nc // 2, nc, rec_step(True), mid)
    for hh in range(hps):
        for d in (0, 1):
            j = 2 * (2 * hh + d)
            cout_ref[0, d, hh] = fin[j]
            nout_ref[0, d, hh] = fin[j + 1]
        h = h_s[hh]
        h = h * lax.rsqrt(jnp.mean(h * h, axis=-1, keepdims=True) + EPS) * dn_ref[...]
        o_ref[:, hh * HD:(hh + 1) * HD] = (h * _sigmoid(xo_ref[hh])).astype(bf16)


def mlstm_mixer(p_rec, gates, gates_t, d_norm, c0, n0, m0, T, row0, hps):
    B = c0.shape[0]
    nc = T // CHUNK
    rb0 = row0 // T

    def xspec(slab0):
        return pl.BlockSpec((hps, T, HD), lambda b, h: (slab0 // hps + h, rb0 + b, 0))

    def gspec(col0):
        return pl.BlockSpec((1, hps, nc, CHUNK), lambda b, h: (b, col0 // hps + h, 0, 0))

    cspec = pl.BlockSpec((1, 2, hps, HD, HD), lambda b, h: (b, 0, h, 0, 0))
    vspec = pl.BlockSpec((1, 2, hps, 1, HD), lambda b, h: (b, 0, h, 0, 0))
    o, cout, nout, mout = pl.pallas_call(
        functools.partial(_mlstm_body, T=T, hps=hps),
        out_shape=(jax.ShapeDtypeStruct((B * T, HEADS * HD), bf16),
                   jax.ShapeDtypeStruct((B, 2, HEADS, HD, HD), f32),
                   jax.ShapeDtypeStruct((B, 2, HEADS, 1, HD), f32),
                   jax.ShapeDtypeStruct((B, 2, HEADS, 1, HD), f32)),
        grid=(B, HEADS // hps),
        in_specs=[pl.BlockSpec(memory_space=pltpu.SMEM),
                  xspec(32), xspec(40), xspec(48), xspec(56),
                  pl.BlockSpec((T, LANES), lambda b, h: (rb0 + b, 0)),
                  gspec(32), gspec(40), gspec(48), gspec(56), cspec, vspec,
                  pl.BlockSpec((1, HD), lambda b, h: (0, 0))],
        out_specs=(pl.BlockSpec((T, hps * HD), lambda b, h: (b, h)), cspec, vspec, vspec),
        scratch_shapes=[pltpu.VMEM((hps, 2, T, CHUNK), bf16), pltpu.VMEM((hps, 2, 2 * T, CHUNK), bf16),
                        pltpu.VMEM((hps, 2, T, HD), f32), pltpu.VMEM((hps, 2, nc * 8, HD), f32),
                        pltpu.VMEM((hps, T, HD), f32)],
        compiler_params=_cparams(("arbitrary", "arbitrary")),
        name="mlstm_mixer",
    )(m0, p_rec, p_rec, p_rec, p_rec, gates, gates_t, gates_t, gates_t, gates_t, c0,
      n0.reshape(B, 2, HEADS, 1, HD), d_norm.reshape(1, HD))
    return o, cout, nout, mout


def _att_weight(w):
    return jnp.concatenate([w[:, :1024], w[:, 1536:], w[:, 1024:1536]], axis=1).astype(bf16)


def _rec_weight(w):
    pad = jnp.zeros((D_MODEL, REC_SLABS * LANES - 8256), w.dtype)
    return jnp.concatenate([w[:, :4096], w[:, 4128:8224], w[:, 4096:4128], w[:, 8224:], pad], axis=1).astype(bf16)


def _gates_rows(gates, row0, B, T):
    g = gates[row0:row0 + B * T, :64].reshape(B, T // CHUNK, CHUNK, 64)
    return jnp.transpose(g, (0, 3, 1, 2))


def kernel(x_prompt, x_sample, cache_a_k, cache_a_v, cache_b_k, cache_b_v, state_c, state_d_c, state_d_n, state_d_m, c, c_ctx, w_mod, b_mod, g_mix_pre, g_mix_post, g_ffn_pre, g_ffn_post, w_in_att, w_in_rec, w_out, a_sink, b_rel_bias, c_conv, c_a_log, c_dt_bias, c_norm, d_b_i, d_b_f, d_norm, w_up, w_down):
    x = jnp.concatenate([x_prompt.reshape(N_CTX, D_MODEL), x_sample.reshape(N_LAT, D_MODEL)], axis=0)
    c_all = jnp.concatenate([c_ctx[None, :], c, jnp.zeros((MOD_ROWS - 1 - DEC_BATCH, D_MODEL), f32)], axis=0)
    mods = adaln_all(c_all, w_mod, b_mod).reshape(DEPTH, MOD_ROWS, 6, D_MODEL)
    tables = _rope_tables()

    ka_c = cache_a_k.reshape(DEC_BATCH, N_ATT, PAST_LEN, A_KV_HEADS * HD)
    va_c = cache_a_v.reshape(DEC_BATCH, N_ATT, PAST_LEN, A_KV_HEADS * HD)
    kb_c = cache_b_k.reshape(DEC_BATCH, N_ATT, PAST_LEN, HEADS * HD)
    vb_c = cache_b_v.reshape(DEC_BATCH, N_ATT, PAST_LEN, HEADS * HD)

    att_kv = []
    rec_states = []
    for l in range(DEPTH):
        j = l // 2
        mod_l = mods[l]
        if l % 2 == 0:
            p, pb = norm_mod_project(x, g_mix_pre[l], mod_l, _att_weight(w_in_att[j]), q_slabs=2 * HEADS)
            att_kv.append(p[16:36, :N_CTX])
            pr = rope_lat(p, tables)
            sink = a_sink[j].astype(f32)
            bias_tab = _na_bias_tables(b_rel_bias[j].astype(f32))
            o_ctx = (ctx_attention_a(pb, sink), ctx_attention_b(pb))
            o_lat = (window_attention_lat(pr, pb, sink, ka_c, va_c, j),
                     neighbourhood_attention_lat(pb, bias_tab, kb_c, vb_c, j))
        else:
            p = norm_mod_project(x, g_mix_pre[l], mod_l, _rec_weight(w_in_rec[j]))
            gates = gate_activations(p, c_a_log[j], c_dt_bias[j], d_b_i[j], d_b_f[j])
            g_ctx = _gates_rows(gates, 0, BATCH, SEQ)
            g_lat = _gates_rows(gates, N_CTX, DEC_BATCH, DEC_SEQ)
            conv_w = jnp.transpose(c_conv[j].astype(f32).reshape(3, 3 * HEADS, HD), (1, 0, 2))
            zc = jnp.zeros((BATCH, 2, HEADS, HD, HD), f32)
            oc_ctx, sc_ctx = delta_mixer(p, gates, g_ctx, conv_w, c_norm[j].astype(f32), zc, SEQ, 0, HPS_CTX)
            oc_lat, _ = delta_mixer(p, gates, g_lat, conv_w, c_norm[j].astype(f32), state_c[:, j].astype(f32),
                                    DEC_SEQ, N_CTX, HPS_LAT)
            od_ctx, cd_ctx, nd_ctx, md_ctx = mlstm_mixer(
                p, gates, g_ctx, d_norm[j].astype(f32), zc, jnp.zeros((BATCH, 2, HEADS, HD), f32),
                jnp.zeros((BATCH, 2, HEADS), f32), SEQ, 0, HPS_CTX)
            od_lat, _, _, _ = mlstm_mixer(
                p, gates, g_lat, d_norm[j].astype(f32), state_d_c[:, j].astype(f32), state_d_n[:, j].astype(f32),
                state_d_m[:, j].astype(f32), DEC_SEQ, N_CTX, HPS_LAT)
            rec_states.append((sc_ctx, cd_ctx, nd_ctx.reshape(BATCH, 2, HEADS, HD), md_ctx[:, :, :, 0, 0]))
            o_ctx = (oc_ctx, od_ctx)
            o_lat = (oc_lat, od_lat)
        x = out_project_residual(o_ctx, o_lat, w_out[l].astype(bf16), x, g_mix_post[l], mod_l)
        x = mlp_residual(x, g_ffn_pre[l], g_ffn_post[l], mod_l, w_up[l].astype(bf16), w_down[l].astype(bf16))

    y_prompt = x[:N_CTX].reshape(BATCH, SEQ, D_MODEL)
    y_sample = x[N_CTX:].reshape(DEC_BATCH, DEC_SEQ, D_MODEL)

    def kv(slab0, n):
        per = [jnp.transpose(a[slab0:slab0 + n].reshape(n, BATCH, SEQ, HD), (1, 2, 0, 3)) for a in att_kv]
        return jnp.stack(per, axis=1)

    new_a_k, new_a_v = kv(16, A_KV_HEADS), kv(18, A_KV_HEADS)
    new_b_k, new_b_v = kv(0, HEADS), kv(8, HEADS)
    new_state_c = jnp.stack([s[0] for s in rec_states], axis=1)
    new_state_d_c = jnp.stack([s[1] for s in rec_states], axis=1)
    new_state_d_n = jnp.stack([s[2] for s in rec_states], axis=1)
    new_state_d_m = jnp.stack([s[3] for s in rec_states], axis=1)
    return (y_prompt, y_sample, new_a_k, new_a_v, new_b_k, new_b_v, new_state_c, new_state_d_c, new_state_d_n,
            new_state_d_m)
```

```python
import functools
import math

import numpy as np
import jax
import jax.numpy as jnp
from jax import lax
from jax.experimental import pallas as pl
from jax.experimental.pallas import tpu as pltpu

f32 = jnp.float32
bf16 = jnp.bfloat16
HI = lax.Precision.HIGHEST

D_MODEL = 2048
BATCH = 16
SEQ = 256
DEPTH = 4
DEC_BATCH = 8
DEC_SEQ = 2048
PAST_LEN = 512
GRID_W = 64
HD = 128
HEADS = 8
A_KV_HEADS = 2
A_REP = 4
FF = 4 * D_MODEL
WINDOW = 128
NA_ROWS = 8
NA_COLS = 16
CHUNK = 64
ROPE_BASE = 10000.0
EPS = 1e-6
N_ATT = 2
N_REC = 2
ATTN_SCALE = HD ** -0.5
GRID_ROWS = DEC_SEQ // GRID_W

N_CTX = BATCH * SEQ
N_LAT = DEC_BATCH * DEC_SEQ
N_ALL = N_CTX + N_LAT
MOD_ROWS = 16

ATT_SLABS = 36
REC_SLABS = 66
LANES = 128
NORM_ROWS = 64
NORM_UNROLL = 2
VMEM_LIMIT = 56 * 1024 * 1024

NT = (((1,), (1,)), ((), ()))
TN_ = (((0,), (0,)), ((), ()))


def _cparams(sem):
    return pltpu.CompilerParams(dimension_semantics=sem, vmem_limit_bytes=VMEM_LIMIT)


def _sigmoid(x):
    return 1.0 / (1.0 + jnp.exp(-x))


def _softplus(x):
    e = jnp.exp(-jnp.abs(x))
    u = 1.0 + e
    log1p = jnp.where(u == 1.0, e, jnp.log(u) * (e / (u - 1.0)))
    return jnp.maximum(x, 0.0) + log1p


def _mod_row(tm):
    ctx_tiles = N_CTX // tm
    per_batch = DEC_SEQ // tm
    return lambda i: jnp.where(i < ctx_tiles, 0, 1 + (i - ctx_tiles) // per_batch)


def _adaln_body(c_ref, w_ref, b_ref, o_ref):
    c = c_ref[...]
    s = (c * _sigmoid(c)).astype(bf16)
    o_ref[0] = jnp.dot(s, w_ref[0].astype(bf16), preferred_element_type=f32) + b_ref[0]


def adaln_all(c_all, w_mod, b_mod):
    tn = 1024
    return pl.pallas_call(
        _adaln_body,
        out_shape=jax.ShapeDtypeStruct((DEPTH, MOD_ROWS, 6 * D_MODEL), f32),
        grid=(DEPTH, 6 * D_MODEL // tn),
        in_specs=[pl.BlockSpec((MOD_ROWS, D_MODEL), lambda l, j: (0, 0)),
                  pl.BlockSpec((1, D_MODEL, tn), lambda l, j: (l, 0, j)),
                  pl.BlockSpec((1, 1, tn), lambda l, j: (l, 0, j))],
        out_specs=pl.BlockSpec((1, MOD_ROWS, tn), lambda l, j: (l, 0, j)),
        compiler_params=_cparams(("arbitrary", "arbitrary")),
        name="adaln",
    )(c_all, w_mod, b_mod.reshape(DEPTH, 1, 6 * D_MODEL))


def _norm_mod_rows(x_ref, g_ref, m_ref, h_ref, tm, sub, sh_row, sc_row):
    gain = g_ref[...] * (1.0 + m_ref[0, sc_row:sc_row + 1, :])
    sh = m_ref[0, sh_row:sh_row + 1, :]

    def body(r, carry):
        rows = pl.ds(pl.multiple_of(r * sub, sub), sub)
        x = x_ref[rows, :]
        ms = jnp.mean(x * x, axis=-1, keepdims=True)
        h_ref[rows, :] = (x * lax.rsqrt(ms + EPS) * gain + sh).astype(bf16)
        return carry

    lax.fori_loop(0, tm // sub, body, 0, unroll=NORM_UNROLL)


def _post_norm_residual_rows(y_ref, x_ref, g_ref, m_ref, tm, sub, gate_row):
    gain = g_ref[...] * m_ref[0, gate_row:gate_row + 1, :]

    def body(r, carry):
        rows = pl.ds(pl.multiple_of(r * sub, sub), sub)
        y = y_ref[rows, :]
        ms = jnp.mean(y * y, axis=-1, keepdims=True)
        y_ref[rows, :] = x_ref[rows, :] + y * lax.rsqrt(ms + EPS) * gain
        return carry

    lax.fori_loop(0, tm // sub, body, 0, unroll=NORM_UNROLL)


def _proj_body(x_ref, g_ref, m_ref, w_ref, o_ref, *rest, tm, tn, sub, q_slabs):
    h_ref = rest[-1]

    @pl.when(pl.program_id(1) == 0)
    def _():
        _norm_mod_rows(x_ref, g_ref, m_ref, h_ref, tm, sub, 0, 1)

    acc = jnp.dot(h_ref[...], w_ref[...], preferred_element_type=f32)
    for s in range(tn // LANES):
        o_ref[s] = acc[:, s * LANES:(s + 1) * LANES]
    if q_slabs:
        ob_ref = rest[0]
        slab0 = pl.program_id(1) * (tn // LANES)
        for s in range(tn // LANES):
            scale = jnp.where(slab0 + s < q_slabs, ATTN_SCALE, 1.0).astype(f32)
            ob_ref[s] = (acc[:, s * LANES:(s + 1) * LANES] * scale).astype(bf16)


def norm_mod_project(x, g_pre, mod_l, w_bf16, q_slabs=0):
    tm, tn, sub = 1024, 768, NORM_ROWS
    ncols = w_bf16.shape[1]
    mrow = _mod_row(tm)
    ospec = pl.BlockSpec((tn // LANES, tm, LANES), lambda i, j: (j, i, 0))
    oshape = jax.ShapeDtypeStruct((ncols // LANES, N_ALL, LANES), f32)
    return pl.pallas_call(
        functools.partial(_proj_body, tm=tm, tn=tn, sub=sub, q_slabs=q_slabs),
        out_shape=(oshape, jax.ShapeDtypeStruct(oshape.shape, bf16)) if q_slabs else oshape,
        grid=(N_ALL // tm, ncols // tn),
        in_specs=[pl.BlockSpec((tm, D_MODEL), lambda i, j: (i, 0)),
                  pl.BlockSpec((1, D_MODEL), lambda i, j: (0, 0)),
                  pl.BlockSpec((1, 6, D_MODEL), lambda i, j: (mrow(i), 0, 0)),
                  pl.BlockSpec((D_MODEL, tn), lambda i, j: (0, j))],
        out_specs=(ospec, ospec) if q_slabs else ospec,
        scratch_shapes=[pltpu.VMEM((tm, D_MODEL), bf16)],
        compiler_params=_cparams(("arbitrary", "arbitrary")),
        name="norm_mod_project",
    )(x, g_pre.reshape(1, D_MODEL), mod_l, w_bf16)


def _outproj_body(c1_ref, c2_ref, l1_ref, l2_ref, w_ref, x_ref, g_ref, m_ref, out_ref, *, tm, sub):
    half = D_MODEL // 2
    is_ctx = pl.program_id(0) < N_CTX // tm

    def project(o1_ref, o2_ref):
        out_ref[...] = (jnp.dot(o1_ref[...], w_ref[0:half, :], preferred_element_type=f32)
                        + jnp.dot(o2_ref[...], w_ref[half:D_MODEL, :], preferred_element_type=f32))

    @pl.when(is_ctx)
    def _():
        project(c1_ref, c2_ref)

    @pl.when(jnp.logical_not(is_ctx))
    def _():
        project(l1_ref, l2_ref)

    _post_norm_residual_rows(out_ref, x_ref, g_ref, m_ref, tm, sub, 2)


def out_project_residual(o_ctx, o_lat, w_bf16, x, g_post, mod_l):
    tm, sub = 512, NORM_ROWS
    half = D_MODEL // 2
    mrow = _mod_row(tm)
    ctx_tiles = N_CTX // tm
    cspec = pl.BlockSpec((tm, half), lambda i: (jnp.minimum(i, ctx_tiles - 1), 0))
    lspec = pl.BlockSpec((tm, half), lambda i: (jnp.maximum(i - ctx_tiles, 0), 0))
    return pl.pallas_call(
        functools.partial(_outproj_body, tm=tm, sub=sub),
        out_shape=jax.ShapeDtypeStruct((N_ALL, D_MODEL), f32),
        grid=(N_ALL // tm,),
        in_specs=[cspec, cspec, lspec, lspec,
                  pl.BlockSpec((D_MODEL, D_MODEL), lambda i: (0, 0)),
                  pl.BlockSpec((tm, D_MODEL), lambda i: (i, 0)),
                  pl.BlockSpec((1, D_MODEL), lambda i: (0, 0)),
                  pl.BlockSpec((1, 6, D_MODEL), lambda i: (mrow(i), 0, 0))],
        out_specs=pl.BlockSpec((tm, D_MODEL), lambda i: (i, 0)),
        compiler_params=_cparams(("arbitrary",)),
        name="out_project_residual",
    )(o_ctx[0], o_ctx[1], o_lat[0], o_lat[1], w_bf16, x, g_post.reshape(1, D_MODEL), mod_l)


def _mlp_body(x_ref, gpre_ref, gpost_ref, m_ref, wu_ref, wd_ref, out_ref, h_ref, *, tm, sub, nj):
    j = pl.program_id(1)

    @pl.when(j == 0)
    def _():
        _norm_mod_rows(x_ref, gpre_ref, m_ref, h_ref, tm, sub, 3, 4)
        out_ref[...] = jnp.zeros_like(out_ref)

    u = jnp.dot(h_ref[...], wu_ref[...], preferred_element_type=f32)
    u = jnp.maximum(u, 0.0)
    u = (u * u).astype(bf16)
    out_ref[...] = jnp.dot(u, wd_ref[...], preferred_element_type=f32) + out_ref[...]

    @pl.when(j == nj - 1)
    def _():
        _post_norm_residual_rows(out_ref, x_ref, gpost_ref, m_ref, tm, sub, 5)


def mlp_residual(x, g_pre, g_post, mod_l, wu_bf16, wd_bf16):
    tm, tf, sub = 512, 1024, NORM_ROWS
    nj = FF // tf
    mrow = _mod_row(tm)
    return pl.pallas_call(
        functools.partial(_mlp_body, tm=tm, sub=sub, nj=nj),
        out_shape=jax.ShapeDtypeStruct((N_ALL, D_MODEL), f32),
        grid=(N_ALL // tm, nj),
        in_specs=[pl.BlockSpec((tm, D_MODEL), lambda i, j: (i, 0)),
                  pl.BlockSpec((1, D_MODEL), lambda i, j: (0, 0)),
                  pl.BlockSpec((1, D_MODEL), lambda i, j: (0, 0)),
                  pl.BlockSpec((1, 6, D_MODEL), lambda i, j: (mrow(i), 0, 0)),
                  pl.BlockSpec((D_MODEL, tf), lambda i, j: (0, j)),
                  pl.BlockSpec((tf, D_MODEL), lambda i, j: (j, 0))],
        out_specs=pl.BlockSpec((tm, D_MODEL), lambda i, j: (i, 0)),
        scratch_shapes=[pltpu.VMEM((tm, D_MODEL), bf16)],
        compiler_params=_cparams(("arbitrary", "arbitrary")),
        name="mlp_residual",
    )(x, g_pre.reshape(1, D_MODEL), g_post.reshape(1, D_MODEL), mod_l, wu_bf16, wd_bf16)


def _rope_tables():
    nf = HD // 4
    t = jnp.arange(DEC_SEQ)
    row = (t // GRID_W).astype(f32)
    col = (t % GRID_W).astype(f32)
    inv = jnp.exp(-math.log(ROPE_BASE) * jnp.arange(nf, dtype=f32) / nf)
    lane = np.arange(HD)
    first_half = (lane % (2 * nf)) < nf
    pos = jnp.where(jnp.asarray(lane < HD // 2)[None, :], row[:, None], col[:, None])
    ang = pos * inv[jnp.asarray(lane % nf)][None, :]
    cos, sin = jnp.cos(ang), jnp.sin(ang)
    fh = jnp.asarray(first_half)[None, :]
    sin_a = jnp.where(fh, -sin, 0.0)
    sin_b = jnp.where(fh, 0.0, sin)

    return cos, sin_a, sin_b


def _rope_body(p_ref, cos_ref, sa_ref, sb_ref, o_ref):
    scale = jnp.where(pl.program_id(0) < HEADS, ATTN_SCALE, 1.0).astype(f32)
    x = p_ref[0]
    y = x * cos_ref[...] + pltpu.roll(x, 96, 1) * sa_ref[...] + pltpu.roll(x, 32, 1) * sb_ref[...]
    o_ref[0] = (y * scale).astype(bf16)


def rope_lat(p_att, tables):
    tm = DEC_SEQ
    ctx_tiles = N_CTX // tm
    cos, sa, sb = tables
    tspec = pl.BlockSpec((tm, HD), lambda s, i: (0, 0))
    return pl.pallas_call(
        _rope_body,
        out_shape=jax.ShapeDtypeStruct((HEADS + A_KV_HEADS, N_LAT, HD), bf16),
        grid=(HEADS + A_KV_HEADS, N_LAT // tm),
        in_specs=[pl.BlockSpec((1, tm, HD), lambda s, i: (jnp.where(s < HEADS, s, s + 24), ctx_tiles + i, 0)),
                  tspec, tspec, tspec],
        out_specs=pl.BlockSpec((1, tm, HD), lambda s, i: (s, i, 0)),
        compiler_params=_cparams(("arbitrary", "arbitrary")),
        name="rope_lat",
    )(p_att, cos, sa, sb)


def _ctx_attn_body(*refs, reps, use_sink):
    if use_sink:
        sink_ref, q_ref, k_ref, v_ref, o_ref = refs
    else:
        q_ref, k_ref, v_ref, o_ref = refs
    g = pl.program_id(1)
    k = k_ref[0]
    v = v_ref[0]
    for r in range(reps):
        s = lax.dot_general(q_ref[r], k, NT, preferred_element_type=f32)
        m = jnp.max(s, axis=-1, keepdims=True)
        if use_sink:
            sk = sink_ref[g, r]
            m = jnp.maximum(m, sk)
        p = jnp.exp(s - m)
        den = jnp.sum(p, axis=-1, keepdims=True)
        if use_sink:
            den = den + jnp.exp(sk - m)
        o = jnp.dot(p.astype(bf16), v, preferred_element_type=f32) / den
        o_ref[:, r * HD:(r + 1) * HD] = o.astype(bf16)


def ctx_attention_a(pb, sink):
    return pl.pallas_call(
        functools.partial(_ctx_attn_body, reps=A_REP, use_sink=True),
        out_shape=jax.ShapeDtypeStruct((N_CTX, HEADS * HD), bf16),
        grid=(BATCH, A_KV_HEADS),
        in_specs=[pl.BlockSpec(memory_space=pltpu.SMEM),
                  pl.BlockSpec((A_REP, SEQ, HD), lambda b, g: (g, b, 0)),
                  pl.BlockSpec((1, SEQ, HD), lambda b, g: (32 + g, b, 0)),
                  pl.BlockSpec((1, SEQ, HD), lambda b, g: (34 + g, b, 0))],
        out_specs=pl.BlockSpec((SEQ, A_REP * HD), lambda b, g: (b, g)),
        compiler_params=_cparams(("arbitrary", "arbitrary")),
        name="ctx_attention_a",
    )(sink, pb, pb, pb)


def ctx_attention_b(pb):
    return pl.pallas_call(
        functools.partial(_ctx_attn_body, reps=1, use_sink=False),
        out_shape=jax.ShapeDtypeStruct((N_CTX, HEADS * HD), bf16),
        grid=(BATCH, HEADS),
        in_specs=[pl.BlockSpec((1, SEQ, HD), lambda b, h: (8 + h, b, 0)),
                  pl.BlockSpec((1, SEQ, HD), lambda b, h: (16 + h, b, 0)),
                  pl.BlockSpec((1, SEQ, HD), lambda b, h: (24 + h, b, 0))],
        out_specs=pl.BlockSpec((SEQ, HD), lambda b, h: (b, h)),
        compiler_params=_cparams(("arbitrary", "arbitrary")),
        name="ctx_attention_b",
    )(pb, pb, pb)


def _window_body(sink_ref, q_ref, k0_ref, k1_ref, k2_ref, v0_ref, v1_ref, v2_ref, kc_ref, vc_ref, o_ref, *, nb):
    g = pl.program_id(1)
    n = pl.program_id(2)
    rows = A_REP * WINDOW
    q = q_ref[...].reshape(rows, HD)
    kc = kc_ref[0, 0].astype(bf16)
    vc = vc_ref[0, 0].astype(bf16)
    s0 = lax.dot_general(q, k0_ref[0], NT, preferred_element_type=f32)
    s1 = lax.dot_general(q, k1_ref[0], NT, preferred_element_type=f32)
    s2 = lax.dot_general(q, k2_ref[0], NT, preferred_element_type=f32)
    sc = lax.dot_general(q, kc, NT, preferred_element_type=f32)
    qi = lax.broadcasted_iota(jnp.int32, (rows, WINDOW), 0) % WINDOW
    kj = lax.broadcasted_iota(jnp.int32, (rows, WINDOW), 1)
    neg = -jnp.inf
    s0 = jnp.where(kj >= qi, s0, neg)
    s0 = jnp.where(n > 0, s0, neg)
    s2 = jnp.where(kj <= qi, s2, neg)
    s2 = jnp.where(n < nb - 1, s2, neg)
    rep = lax.broadcasted_iota(jnp.int32, (rows, 1), 0) // WINDOW
    sk = jnp.zeros((rows, 1), f32)
    for r in range(A_REP):
        sk = jnp.where(rep == r, sink_ref[g, r], sk)
    m = jnp.maximum(jnp.maximum(jnp.max(s0, axis=-1, keepdims=True), jnp.max(s1, axis=-1, keepdims=True)),
                    jnp.maximum(jnp.max(s2, axis=-1, keepdims=True), jnp.max(sc, axis=-1, keepdims=True)))
    m = jnp.maximum(m, sk)
    p0 = jnp.exp(s0 - m)
    p1 = jnp.exp(s1 - m)
    p2 = jnp.exp(s2 - m)
    pc = jnp.exp(sc - m)
    den = (jnp.sum(p0, axis=-1, keepdims=True) + jnp.sum(p1, axis=-1, keepdims=True)
           + jnp.sum(p2, axis=-1, keepdims=True) + jnp.sum(pc, axis=-1, keepdims=True) + jnp.exp(sk - m))
    o = (jnp.dot(p0.astype(bf16), v0_ref[0], preferred_element_type=f32)
         + jnp.dot(p1.astype(bf16), v1_ref[0], preferred_element_type=f32)
         + jnp.dot(p2.astype(bf16), v2_ref[0], preferred_element_type=f32)
         + jnp.dot(pc.astype(bf16), vc, preferred_element_type=f32)) / den
    for r in range(A_REP):
        o_ref[:, r * HD:(r + 1) * HD] = o[r * WINDOW:(r + 1) * WINDOW].astype(bf16)


def window_attention_lat(pr, pb, sink, ka_c, va_c, j):
    nb = DEC_SEQ // WINDOW

    def qrow(b, n):
        return b * nb + n

    def kspec(slab0, shift, base=0):
        return pl.BlockSpec((1, WINDOW, HD),
                            lambda b, g, n: (slab0 + g, base + qrow(b, jnp.clip(n + shift, 0, nb - 1)), 0))

    vbase = N_CTX // WINDOW

    cspec = pl.BlockSpec((1, 1, PAST_LEN, HD), lambda b, g, n: (b, j, 0, g))
    return pl.pallas_call(
        functools.partial(_window_body, nb=nb),
        out_shape=jax.ShapeDtypeStruct((N_LAT, HEADS * HD), bf16),
        grid=(DEC_BATCH, A_KV_HEADS, nb),
        in_specs=[pl.BlockSpec(memory_space=pltpu.SMEM),
                  pl.BlockSpec((A_REP, WINDOW, HD), lambda b, g, n: (g, qrow(b, n), 0)),
                  kspec(HEADS, -1), kspec(HEADS, 0), kspec(HEADS, 1),
                  kspec(34, -1, vbase), kspec(34, 0, vbase), kspec(34, 1, vbase),
                  cspec, cspec],
        out_specs=pl.BlockSpec((WINDOW, A_REP * HD), lambda b, g, n: (b * nb + n, g)),
        compiler_params=_cparams(("arbitrary", "arbitrary", "arbitrary")),
        name="window_attention",
    )(sink, pr, pr, pr, pr, pb, pb, pb, ka_c, va_c)


NA_QROWS = 2
NA_WROWS = NA_QROWS + NA_ROWS - 1
NA_WIN_MAX = GRID_ROWS - NA_WROWS
NA_OFFS = (0, -2, -4, -5, -7)
NA_HEAD_GROUP = 4
NA_Q = NA_QROWS * GRID_W
NA_K = NA_WROWS * GRID_W


def _na_bias_tables(rel_bias):
    r0_of = {0: 0, -2: 2, -4: 4, -5: 28, -7: 30}
    cq = np.arange(GRID_W)[:, None]
    ck = np.arange(GRID_W)[None, :]
    wstart = np.clip(cq - NA_COLS // 2, 0, GRID_W - NA_COLS)
    col_ok = (ck >= wstart) & (ck < wstart + NA_COLS)
    dc = np.clip(ck - cq + NA_COLS - 1, 0, 2 * NA_COLS - 2)
    pick = ((dc[None] == np.arange(2 * NA_COLS - 1)[:, None, None]) & col_ok[None]).astype(np.float32)
    toep = jnp.einsum("hrd,dqk->hrqk", rel_bias, jnp.asarray(pick), precision=HI)
    toep = jnp.where(jnp.asarray(col_ok)[None, None], toep, -1e30)
    masked = jnp.full((HEADS, GRID_W, GRID_W), -1e30, f32)
    tables = []
    for off in NA_OFFS:
        r0 = r0_of[off]
        ws = int(np.clip(r0 - NA_ROWS // 2, 0, NA_WIN_MAX))
        assert ws - r0 == off
        qrows = []
        for a in range(NA_QROWS):
            r = r0 + a
            rs = int(np.clip(r - NA_ROWS // 2, 0, GRID_ROWS - NA_ROWS))
            blocks = []
            for w in range(NA_WROWS):
                krow = ws + w
                blocks.append(toep[:, krow - r + NA_ROWS - 1] if rs <= krow < rs + NA_ROWS else masked)
            qrows.append(jnp.concatenate(blocks, axis=2))
        tables.append(jnp.concatenate(qrows, axis=1))
    return jnp.stack(tables).astype(f32)


def _na_body(bias_ref, q_ref, k_ref, v_ref, kc_ref, vc_ref, o_ref, kcb_ref, vcb_ref):
    rb = pl.program_id(1)

    @pl.when(rb == 0)
    def _():
        kcb_ref[...] = kc_ref[0, 0].astype(bf16)
        vcb_ref[...] = vc_ref[0, 0].astype(bf16)

    ws = jnp.clip(NA_QROWS * rb - NA_ROWS // 2, 0, NA_WIN_MAX)
    win = pl.ds(pl.multiple_of(ws * GRID_W, GRID_W), NA_K)
    def lanes(h):
        return slice(h * HD, (h + 1) * HD)

    for h0 in range(0, HEADS, NA_HEAD_GROUP):
        hs = range(h0, h0 + NA_HEAD_GROUP)
        sl = [lax.dot_general(q_ref[h], k_ref[h, win, :], NT, preferred_element_type=f32) + bias_ref[0, h] for h in hs]
        sc = [lax.dot_general(q_ref[h], kcb_ref[:, lanes(h)], NT, preferred_element_type=f32) for h in hs]
        m = [jnp.maximum(jnp.max(a, axis=-1, keepdims=True), jnp.max(b, axis=-1, keepdims=True))
             for a, b in zip(sl, sc)]
        p_l = [jnp.exp(a - x) for a, x in zip(sl, m)]
        p_c = [jnp.exp(b - x) for b, x in zip(sc, m)]
        den = [jnp.sum(a, axis=-1, keepdims=True) + jnp.sum(b, axis=-1, keepdims=True) for a, b in zip(p_l, p_c)]
        o = [jnp.dot(a.astype(bf16), v_ref[h, win, :], preferred_element_type=f32)
             + jnp.dot(b.astype(bf16), vcb_ref[:, lanes(h)], preferred_element_type=f32)
             for h, a, b in zip(hs, p_l, p_c)]
        for h, x, d in zip(hs, o, den):
            o_ref[:, lanes(h)] = (x / d).astype(bf16)


def neighbourhood_attention_lat(pb, bias_tab, kb_c, vb_c, j):
    nrb = GRID_ROWS // NA_QROWS
    base = N_CTX // NA_Q
    per_b = DEC_SEQ // NA_Q
    seq_base = N_CTX // DEC_SEQ

    def tbl(rb):
        return jnp.where(rb < 2, rb, jnp.where(rb < nrb - 2, 2, rb - (nrb - 5)))

    cspec = pl.BlockSpec((1, 1, PAST_LEN, HEADS * HD), lambda b, rb: (b, j, 0, 0))
    return pl.pallas_call(
        _na_body,
        out_shape=jax.ShapeDtypeStruct((N_LAT, HEADS * HD), bf16),
        grid=(DEC_BATCH, nrb),
        in_specs=[pl.BlockSpec((1, HEADS, NA_Q, NA_K), lambda b, rb: (tbl(rb), 0, 0, 0)),
                  pl.BlockSpec((HEADS, NA_Q, HD), lambda b, rb: (1, base + b * per_b + rb, 0)),
                  pl.BlockSpec((HEADS, DEC_SEQ, HD), lambda b, rb: (2, seq_base + b, 0)),
                  pl.BlockSpec((HEADS, DEC_SEQ, HD), lambda b, rb: (3, seq_base + b, 0)),
                  cspec, cspec],
        out_specs=pl.BlockSpec((NA_Q, HEADS * HD), lambda b, rb: (b * per_b + rb, 0)),
        scratch_shapes=[pltpu.VMEM((PAST_LEN, HEADS * HD), bf16), pltpu.VMEM((PAST_LEN, HEADS * HD), bf16)],
        compiler_params=_cparams(("arbitrary", "arbitrary")),
        name="neighbourhood_attention",
    )(bias_tab, pb, pb, pb, kb_c, vb_c)


def _gates_body(p_ref, par_ref, o_ref, *, tm, blk):
    lane = lax.broadcasted_iota(jnp.int32, (blk, LANES), 1)
    ii = lax.broadcasted_iota(jnp.int32, (blk, blk), 0)
    jj = lax.broadcasted_iota(jnp.int32, (blk, blk), 1)
    same = (ii // CHUNK) == (jj // CHUNK)
    lower = jnp.where(same & (jj <= ii), 1.0, 0.0).astype(f32)
    upper = jnp.where(same & (jj >= ii), 1.0, 0.0).astype(f32)
    cumulated = ((lane >= 16) & (lane < 32)) | ((lane >= 48) & (lane < 64))
    backward = (lane % 16) >= 8
    for r in range(tm // blk):
        x = p_ref[0, r * blk:(r + 1) * blk, :]
        z = x + par_ref[1:2, :]
        act = jnp.where(lane < 16, _sigmoid(z),
                        jnp.where(lane < 32, par_ref[0:1, :] * _softplus(z), jnp.where(lane < 48, z, -_softplus(-z))))
        pre = jnp.dot(lower, act, precision=HI, preferred_element_type=f32)
        suf = jnp.dot(upper, act, precision=HI, preferred_element_type=f32)
        o_ref[r * blk:(r + 1) * blk, :] = jnp.where(cumulated, jnp.where(backward, suf, pre), act)


def gate_activations(p_rec, a_log, dt_bias, b_i, b_f):
    tm = 1024
    zeros16 = jnp.zeros((16,), f32)
    neg_a = jnp.concatenate([zeros16, -jnp.exp(a_log.reshape(16).astype(f32)), zeros16, zeros16, jnp.zeros((64,), f32)])
    bias = jnp.concatenate([zeros16, dt_bias.reshape(16).astype(f32), b_i.reshape(16).astype(f32),
                            b_f.reshape(16).astype(f32), jnp.zeros((64,), f32)])
    par = jnp.concatenate([neg_a[None], bias[None], jnp.zeros((6, LANES), f32)], axis=0)
    return pl.pallas_call(
        functools.partial(_gates_body, tm=tm, blk=4 * CHUNK),
        out_shape=jax.ShapeDtypeStruct((N_ALL, LANES), f32),
        grid=(N_ALL // tm,),
        in_specs=[pl.BlockSpec((1, tm, LANES), lambda i: (64, i, 0)),
                  pl.BlockSpec((8, LANES), lambda i: (0, 0))],
        out_specs=pl.BlockSpec((tm, LANES), lambda i: (i, 0)),
        compiler_params=_cparams(("arbitrary",)),
        name="gate_activations",
    )(p_rec, par)


def _tri(reverse, strict):
    ii = lax.broadcasted_iota(jnp.int32, (CHUNK, CHUNK), 0)
    jj = lax.broadcasted_iota(jnp.int32, (CHUNK, CHUNK), 1)
    if reverse:
        return (jj > ii) if strict else (jj >= ii)
    return (jj < ii) if strict else (jj <= ii)


def _split(x):
    hi = x.astype(bf16)
    return hi, (x - hi.astype(f32)).astype(bf16)


def _mm(a, b):
    return jnp.dot(a.astype(bf16), b.astype(bf16), preferred_element_type=f32)


def _unit_tri_inverses(mats):
    ii = lax.broadcasted_iota(jnp.int32, (CHUNK, CHUNK), 0)
    jj = lax.broadcasted_iota(jnp.int32, (CHUNK, CHUNK), 1)
    same16 = (ii // 16) == (jj // 16)
    same32 = (ii // 32) == (jj // 32)
    eye = jnp.where(ii == jj, 1.0, 0.0).astype(f32)
    m = [jnp.where(same16, -a, 0.0) for a in mats]
    l1 = [jnp.where(same16, 0.0, jnp.where(same32, a, 0.0)) for a in mats]
    l2 = [jnp.where(same32, 0.0, a) for a in mats]
    p = [eye + x for x in m]
    m = [_mm(x, x) for x in m]
    for _ in range(2):
        r = [_mm(jnp.concatenate([x, y], axis=0), x) for x, y in zip(m, p)]
        m = [x[:CHUNK] for x in r]
        p = [y + x[CHUNK:] for x, y in zip(r, p)]
    p = [y + _mm(y, x) for x, y in zip(m, p)]
    p = [y - _mm(y, _mm(l, y)) for l, y in zip(l1, p)]
    return [y - _mm(y, _mm(l, y)) for l, y in zip(l2, p)]


def _gate_cols(gt_ref, rows, lane_ids):
    x = gt_ref[rows, :]
    lane = lax.broadcasted_iota(jnp.int32, x.shape, 1)
    return [jnp.sum(jnp.where(lane == i, x, 0.0), axis=-1, keepdims=True) for i in lane_ids]


DELTA_PREP_CHUNKS = 16
MLSTM_PREP_CHUNKS = 4
HPS_CTX = 4
HPS_LAT = 2


def _chunk_rows(c):
    return pl.ds(pl.multiple_of(c * CHUNK, CHUNK), CHUNK)


def _chunk_pair_rows(c):
    return pl.ds(pl.multiple_of(c * 2 * CHUNK, 2 * CHUNK), 2 * CHUNK)


def _chunk_stat_rows(c):
    return pl.ds(pl.multiple_of(c * 8, 8), 8)


def _conv_silu(x_ref, cw_ref, T):
    x = x_ref[...]
    w = cw_ref[...]
    row = lax.broadcasted_iota(jnp.int32, (T, HD), 0)
    xp = jnp.where(row == 0, 0.0, pltpu.roll(x, 1, 0))
    xn = jnp.where(row == T - 1, 0.0, pltpu.roll(x, T - 1, 0))
    y = w[0:1] * xp + w[1:2] * x + w[2:3] * xn
    return y * _sigmoid(y)


def _delta_prepare(head, xq_ref, xk_ref, xv_ref, cwq_ref, cwk_ref, cwv_ref, gt_ref, gcum_rows,
                   q_s, k_s, v_s, u_s, wq_s, kd_s, at_s, eg_s, T):
    nc = T // CHUNK
    q = _conv_silu(xq_ref, cwq_ref, T)
    q_s[...] = q * lax.rsqrt(jnp.sum(q * q, axis=-1, keepdims=True) + EPS) * ATTN_SCALE
    k = _conv_silu(xk_ref, cwk_ref, T)
    k_s[...] = k * lax.rsqrt(jnp.sum(k * k, axis=-1, keepdims=True) + EPS)
    v_s[...] = _conv_silu(xv_ref, cwv_ref, T)
    incl = (_tri(False, False), _tri(True, False))
    strict = (_tri(False, True), _tri(True, True))
    group = min(DELTA_PREP_CHUNKS, nc)

    def prep_group(t, carry):
        chains = [(t * group + i, d) for i in range(group) for d in (0, 1)]
        ld = []
        for c, d in chains:
            rows = _chunk_rows(c)
            beta_c, gcum_c = _gate_cols(gt_ref, rows, (d * HEADS + head, 2 * HEADS + d * HEADS + head))
            ld.append((q_s[rows, :], k_s[rows, :], v_s[rows, :], gcum_rows[d][pl.ds(c, 1), :], gcum_c, beta_c))
        decay = [jnp.exp(jnp.where(incl[d], x[4] - x[3], -jnp.inf)) for (c, d), x in zip(chains, ld)]
        kb = [x[1] * x[5] for x in ld]
        a = [jnp.where(strict[d], lax.dot_general(y.astype(bf16), x[1].astype(bf16), NT,
                                                  preferred_element_type=f32) * dc, 0.0)
             for (c, d), x, y, dc in zip(chains, ld, kb, decay)]
        inv = _unit_tri_inverses(a)
        for (c, d), (q, k, v, gcum_r, gcum_c, beta_c), kbi, dc, t_inv in zip(chains, ld, kb, decay, inv):
            rows = _chunk_rows(c)
            eg_c = jnp.exp(gcum_c)
            rhs_hi, rhs_lo = _split(jnp.concatenate([v * beta_c, kbi * eg_c], axis=1))
            t_inv = t_inv.astype(bf16)
            uw = (jnp.dot(t_inv, rhs_hi, preferred_element_type=f32)
                  + jnp.dot(t_inv, rhs_lo, preferred_element_type=f32))
            u_s[d, rows, :] = uw[:, :HD]
            wq_s[d, _chunk_pair_rows(c), :] = jnp.concatenate([uw[:, HD:], q * eg_c], axis=0).astype(bf16)
            att = lax.dot_general(q.astype(bf16), k.astype(bf16), NT, preferred_element_type=f32) * dc
            at_s[d, rows, :] = att.astype(bf16)
            glast = gcum_r[:, 0:1] if d == 1 else gcum_r[:, CHUNK - 1:CHUNK]
            kd_s[d, rows, :] = (k * jnp.exp(glast - gcum_c)).astype(bf16)
            eg_s[d, _chunk_stat_rows(c), :] = jnp.broadcast_to(jnp.exp(glast), (8, HD))
        return carry

    lax.fori_loop(0, nc // group, prep_group, 0)


def _delta_body(xq_ref, xk_ref, xv_ref, xz_ref, cwq_ref, cwk_ref, cwv_ref, gt_ref, gf_ref, gb_ref,
                s0_ref, cn_ref, o_ref, sout_ref,
                q_s, k_s, v_s, u_s, wq_s, kd_s, at_s, eg_s, o_s, *, T, hps):
    nc = T // CHUNK
    for hh in range(hps):
        _delta_prepare(pl.program_id(1) * hps + hh, xq_ref.at[hh], xk_ref.at[hh], xv_ref.at[hh],
                       cwq_ref.at[hh], cwk_ref.at[hh], cwv_ref.at[hh], gt_ref,
                       (gf_ref.at[0, hh], gb_ref.at[0, hh]),
                       q_s.at[hh], k_s.at[hh], v_s.at[hh], u_s.at[hh], wq_s.at[hh], kd_s.at[hh], at_s.at[hh],
                       eg_s.at[hh], T)

    chains = [(hh, d) for hh in range(hps) for d in (0, 1)]

    def rec_step(accumulate):
        def step(i, carry):
            cs = [i if d == 0 else nc - 1 - i for hh, d in chains]
            r = [jnp.dot(wq_s[hh, d, _chunk_pair_rows(c), :], s.astype(bf16), preferred_element_type=f32)
                 for (hh, d), c, s in zip(chains, cs, carry)]
            vn = [(u_s[hh, d, _chunk_rows(c), :] - x[:CHUNK]).astype(bf16) for (hh, d), c, x in zip(chains, cs, r)]
            o = [x[CHUNK:] + jnp.dot(at_s[hh, d, _chunk_rows(c), :], y, preferred_element_type=f32)
                 for (hh, d), c, x, y in zip(chains, cs, r, vn)]
            upd = [lax.dot_general(kd_s[hh, d, _chunk_rows(c), :], y, TN_, preferred_element_type=f32)
                   for (hh, d), c, y in zip(chains, cs, vn)]
            for (hh, d), c, x in zip(chains, cs, o):
                if accumulate:
                    o_s[hh, _chunk_rows(c), :] += x
                else:
                    o_s[hh, _chunk_rows(c), :] = x
            return tuple(s * eg_s[hh, d, _chunk_stat_rows(c), :][0:1] + x
                         for (hh, d), c, s, x in zip(chains, cs, carry, upd))
        return step

    init = tuple(s0_ref[0, d, hh] for hh in range(hps) for d in (0, 1))
    mid = lax.fori_loop(0, nc // 2, rec_step(False), init)
    fin = lax.fori_loop(nc // 2, nc, rec_step(True), mid)
    for hh in range(hps):
        sout_ref[0, 0, hh] = fin[2 * hh]
        sout_ref[0, 1, hh] = fin[2 * hh + 1]
        o = o_s[hh]
        o = o * lax.rsqrt(jnp.mean(o * o, axis=-1, keepdims=True) + EPS) * cn_ref[...]
        z = xz_ref[hh]
        o_ref[:, hh * HD:(hh + 1) * HD] = (o * (z * _sigmoid(z))).astype(bf16)


def delta_mixer(p_rec, gates, gates_t, conv_w, c_norm, s0, T, row0, hps):
    B = s0.shape[0]
    nc = T // CHUNK
    rb0 = row0 // T

    def xspec(slab0):
        return pl.BlockSpec((hps, T, HD), lambda b, h: (slab0 // hps + h, rb0 + b, 0))

    def cwspec(slab0):
        return pl.BlockSpec((hps, 3, HD), lambda b, h: (slab0 // hps + h, 0, 0))

    def gspec(col0):
        return pl.BlockSpec((1, hps, nc, CHUNK), lambda b, h: (b, col0 // hps + h, 0, 0))

    sspec = pl.BlockSpec((1, 2, hps, HD, HD), lambda b, h: (b, 0, h, 0, 0))
    o, sout = pl.pallas_call(
        functools.partial(_delta_body, T=T, hps=hps),
        out_shape=(jax.ShapeDtypeStruct((B * T, HEADS * HD), bf16),
                   jax.ShapeDtypeStruct((B, 2, HEADS, HD, HD), f32)),
        grid=(B, HEADS // hps),
        in_specs=[xspec(0), xspec(8), xspec(16), xspec(24), cwspec(0), cwspec(8), cwspec(16),
                  pl.BlockSpec((T, LANES), lambda b, h: (rb0 + b, 0)),
                  gspec(16), gspec(24), sspec,
                  pl.BlockSpec((1, HD), lambda b, h: (0, 0))],
        out_specs=(pl.BlockSpec((T, hps * HD), lambda b, h: (b, h)), sspec),
        scratch_shapes=[pltpu.VMEM((hps, T, HD), f32), pltpu.VMEM((hps, T, HD), f32), pltpu.VMEM((hps, T, HD), f32),
                        pltpu.VMEM((hps, 2, T, HD), f32), pltpu.VMEM((hps, 2, 2 * T, HD), bf16),
                        pltpu.VMEM((hps, 2, T, HD), bf16), pltpu.VMEM((hps, 2, T, CHUNK), bf16),
                        pltpu.VMEM((hps, 2, nc * 8, HD), f32), pltpu.VMEM((hps, T, HD), f32)],
        compiler_params=_cparams(("arbitrary", "arbitrary")),
        name="delta_mixer",
    )(p_rec, p_rec, p_rec, p_rec, conv_w, conv_w, conv_w, gates, gates_t, gates_t, s0,
      c_norm.reshape(1, HD))
    return o, sout


def _mlstm_prepare(head, m0, xq_ref, xk_ref, gt_ref, i_rows, b_rows, qk_s, kw_s, col_s, ds_s, T):
    nc = T // CHUNK
    incl = (_tri(False, False), _tri(True, False))
    lane = lax.broadcasted_iota(jnp.int32, (CHUNK, HD), 1)
    row8 = lax.broadcasted_iota(jnp.int32, (8, HD), 0)
    eye = jnp.where(lax.broadcasted_iota(jnp.int32, (HD, HD), 0) == lax.broadcasted_iota(jnp.int32, (HD, HD), 1),
                    1.0, 0.0).astype(bf16)
    group = min(MLSTM_PREP_CHUNKS, nc)

    def prep_group(t, carry):
        chains = [(t * group + i if d == 0 else nc - 1 - (t * group + i), d)
                  for i in range(group) for d in (0, 1)]
        ld = []
        for c, d in chains:
            rows = _chunk_rows(c)
            q = (xq_ref[rows, :] * ATTN_SCALE).astype(bf16)
            k = xk_ref[rows, :]
            i_r = i_rows[d][pl.ds(c, 1), :]
            b_r = b_rows[d][pl.ds(c, 1), :]
            i_c, b_c = _gate_cols(gt_ref, rows, (4 * HEADS + d * HEADS + head, 6 * HEADS + d * HEADS + head))
            dmat = jnp.where(incl[d], b_c - b_r + i_r, -jnp.inf)
            dmax = jnp.max(dmat, axis=-1, keepdims=True)
            qk = lax.dot_general(q, k.astype(bf16), NT, preferred_element_type=f32)
            ld.append((k, b_r, b_c, i_c, dmat, dmax, qk))
        m = list(carry)
        for (c, d), (k, b_r, b_c, i_c, dmat, dmax, qk) in zip(chains, ld):
            last = 0 if d == 1 else CHUNK - 1
            rows = _chunk_rows(c)
            inter = b_c + m[d]
            mt = jnp.maximum(inter, dmax)
            qkw = qk * jnp.exp(dmat - mt)
            m_new = mt[last:last + 1, :]
            b_last = b_r[:, last:last + 1]
            kw = k * jnp.exp(b_last - b_c + i_c - m_new)
            qk_s[d, rows, :] = qkw.astype(bf16)
            kw_s[d, _chunk_pair_rows(c), :] = lax.dot_general(eye, kw.astype(bf16), NT,
                                                              preferred_element_type=f32).astype(bf16)
            col_s[d, rows, :] = jnp.where(lane == 0, jnp.exp(inter - mt),
                                          jnp.where(lane == 1, jnp.sum(qkw, axis=-1, keepdims=True), jnp.exp(-mt)))
            ds = jnp.broadcast_to(jnp.exp(b_last + m[d] - m_new), (8, HD))
            ksum = jnp.broadcast_to(jnp.sum(kw, axis=0, keepdims=True), (8, HD))
            ds_s[d, _chunk_stat_rows(c), :] = jnp.where(row8 == 0, ds, ksum)
            m[d] = m_new
        return tuple(m)

    return lax.fori_loop(0, nc // group, prep_group, m0)


def _mlstm_body(m0_ref, xq_ref, xk_ref, xv_ref, xo_ref, gt_ref, if_ref, ib_ref, ff_ref, fb_ref, c0_ref, n0_ref,
                dn_ref, o_ref, cout_ref, nout_ref, mout_ref,
                qk_s, kw_s, col_s, ds_s, h_s, *, T, hps):
    nc = T // CHUNK
    b_id = pl.program_id(0)
    for hh in range(hps):
        head = pl.program_id(1) * hps + hh
        m0 = (jnp.full((1, 1), m0_ref[b_id, 0, head], f32), jnp.full((1, 1), m0_ref[b_id, 1, head], f32))
        mf, mb = _mlstm_prepare(head, m0, xq_ref.at[hh], xk_ref.at[hh], gt_ref,
                                (if_ref.at[0, hh], ib_ref.at[0, hh]), (ff_ref.at[0, hh], fb_ref.at[0, hh]),
                                qk_s.at[hh], kw_s.at[hh], col_s.at[hh], ds_s.at[hh], T)
        mout_ref[0, 0, hh] = jnp.broadcast_to(mf, (1, HD))
        mout_ref[0, 1, hh] = jnp.broadcast_to(mb, (1, HD))

    chains = [(hh, d) for hh in range(hps) for d in (0, 1)]

    def rec_step(accumulate):
        def step(i, carry):
            cs = [i if d == 0 else nc - 1 - i for hh, d in chains]
            cms, ns = carry[0::2], carry[1::2]
            q = [xq_ref[hh, _chunk_rows(c), :] * ATTN_SCALE for (hh, d), c in zip(chains, cs)]
            v = [xv_ref[hh, _chunk_rows(c), :].astype(bf16) for (hh, d), c in zip(chains, cs)]
            qc = [jnp.dot(x.astype(bf16), cm.astype(bf16), preferred_element_type=f32) for x, cm in zip(q, cms)]
            qv = [jnp.dot(qk_s[hh, d, _chunk_rows(c), :], y, preferred_element_type=f32)
                  for (hh, d), c, y in zip(chains, cs, v)]
            upd = [jnp.dot(kw_s[hh, d, _chunk_pair_rows(c), :], y, preferred_element_type=f32)
                   for (hh, d), c, y in zip(chains, cs, v)]
            out = []
            for (hh, d), c, x, a, b, u, cm, n in zip(chains, cs, q, qc, qv, upd, cms, ns):
                rows = _chunk_rows(c)
                cols = col_s[hh, d, rows, :]
                w_inter = cols[:, 0:1]
                den = w_inter * jnp.sum(x * n, axis=-1, keepdims=True) + cols[:, 1:2]
                h = (w_inter * a + b) / jnp.maximum(jnp.abs(den), cols[:, 2:3])
                if accumulate:
                    h_s[hh, rows, :] += h
                else:
                    h_s[hh, rows, :] = h
                dsn = ds_s[hh, d, _chunk_stat_rows(c), :]
                out.extend((dsn[0:1] * cm + u, dsn[0:1] * n + dsn[1:2]))
            return tuple(out)
        return step

    init = []
    for hh in range(hps):
        for d in (0, 1):
            init.extend((c0_ref[0, d, hh], n0_ref[0, d, hh]))
    mid = lax.fori_loop(0, nc // 2, rec_step(False), tuple(init))
    fin = lax.fori_loop(nc // 2, nc, rec_step(True), mid)
    for hh in range(hps):
        for d in (0, 1):
            j = 2 * (2 * hh + d)
            cout_ref[0, d, hh] = fin[j]
            nout_ref[0, d, hh] = fin[j + 1]
        h = h_s[hh]
        h = h * lax.rsqrt(jnp.mean(h * h, axis=-1, keepdims=True) + EPS) * dn_ref[...]
        o_ref[:, hh * HD:(hh + 1) * HD] = (h * _sigmoid(xo_ref[hh])).astype(bf16)


def mlstm_mixer(p_rec, gates, gates_t, d_norm, c0, n0, m0, T, row0, hps):
    B = c0.shape[0]
    nc = T // CHUNK
    rb0 = row0 // T

    def xspec(slab0):
        return pl.BlockSpec((hps, T, HD), lambda b, h: (slab0 // hps + h, rb0 + b, 0))

    def gspec(col0):
        return pl.BlockSpec((1, hps, nc, CHUNK), lambda b, h: (b, col0 // hps + h, 0, 0))

    cspec = pl.BlockSpec((1, 2, hps, HD, HD), lambda b, h: (b, 0, h, 0, 0))
    vspec = pl.BlockSpec((1, 2, hps, 1, HD), lambda b, h: (b, 0, h, 0, 0))
    o, cout, nout, mout = pl.pallas_call(
        functools.partial(_mlstm_body, T=T, hps=hps),
        out_shape=(jax.ShapeDtypeStruct((B * T, HEADS * HD), bf16),
                   jax.ShapeDtypeStruct((B, 2, HEADS, HD, HD), f32),
                   jax.ShapeDtypeStruct((B, 2, HEADS, 1, HD), f32),
                   jax.ShapeDtypeStruct((B, 2, HEADS, 1, HD), f32)),
        grid=(B, HEADS // hps),
        in_specs=[pl.BlockSpec(memory_space=pltpu.SMEM),
                  xspec(32), xspec(40), xspec(48), xspec(56),
                  pl.BlockSpec((T, LANES), lambda b, h: (rb0 + b, 0)),
                  gspec(32), gspec(40), gspec(48), gspec(56), cspec, vspec,
                  pl.BlockSpec((1, HD), lambda b, h: (0, 0))],
        out_specs=(pl.BlockSpec((T, hps * HD), lambda b, h: (b, h)), cspec, vspec, vspec),
        scratch_shapes=[pltpu.VMEM((hps, 2, T, CHUNK), bf16), pltpu.VMEM((hps, 2, 2 * T, CHUNK), bf16),
                        pltpu.VMEM((hps, 2, T, HD), f32), pltpu.VMEM((hps, 2, nc * 8, HD), f32),
                        pltpu.VMEM((hps, T, HD), f32)],
        compiler_params=_cparams(("arbitrary", "arbitrary")),
        name="mlstm_mixer",
    )(m0, p_rec, p_rec, p_rec, p_rec, gates, gates_t, gates_t, gates_t, gates_t, c0,
      n0.reshape(B, 2, HEADS, 1, HD), d_norm.reshape(1, HD))
    return o, cout, nout, mout


def _att_weight(w):
    return jnp.concatenate([w[:, :1024], w[:, 1536:], w[:, 1024:1536]], axis=1).astype(bf16)


def _rec_weight(w):
    pad = jnp.zeros((D_MODEL, REC_SLABS * LANES - 8256), w.dtype)
    return jnp.concatenate([w[:, :4096], w[:, 4128:8224], w[:, 4096:4128], w[:, 8224:], pad], axis=1).astype(bf16)


def _gates_rows(gates, row0, B, T):
    g = gates[row0:row0 + B * T, :64].reshape(B, T // CHUNK, CHUNK, 64)
    return jnp.transpose(g, (0, 3, 1, 2))


def kernel(x_prompt, x_sample, cache_a_k, cache_a_v, cache_b_k, cache_b_v, state_c, state_d_c, state_d_n, state_d_m, c, c_ctx, w_mod, b_mod, g_mix_pre, g_mix_post, g_ffn_pre, g_ffn_post, w_in_att, w_in_rec, w_out, a_sink, b_rel_bias, c_conv, c_a_log, c_dt_bias, c_norm, d_b_i, d_b_f, d_norm, w_up, w_down):
    x = jnp.concatenate([x_prompt.reshape(N_CTX, D_MODEL), x_sample.reshape(N_LAT, D_MODEL)], axis=0)
    c_all = jnp.concatenate([c_ctx[None, :], c, jnp.zeros((MOD_ROWS - 1 - DEC_BATCH, D_MODEL), f32)], axis=0)
    mods = adaln_all(c_all, w_mod, b_mod).reshape(DEPTH, MOD_ROWS, 6, D_MODEL)
    tables = _rope_tables()

    ka_c = cache_a_k.reshape(DEC_BATCH, N_ATT, PAST_LEN, A_KV_HEADS * HD)
    va_c = cache_a_v.reshape(DEC_BATCH, N_ATT, PAST_LEN, A_KV_HEADS * HD)
    kb_c = cache_b_k.reshape(DEC_BATCH, N_ATT, PAST_LEN, HEADS * HD)
    vb_c = cache_b_v.reshape(DEC_BATCH, N_ATT, PAST_LEN, HEADS * HD)

    att_kv = []
    rec_states = []
    for l in range(DEPTH):
        j = l // 2
        mod_l = mods[l]
        if l % 2 == 0:
            p, pb = norm_mod_project(x, g_mix_pre[l], mod_l, _att_weight(w_in_att[j]), q_slabs=2 * HEADS)
            att_kv.append(p[16:36, :N_CTX])
            pr = rope_lat(p, tables)
            sink = a_sink[j].astype(f32)
            bias_tab = _na_bias_tables(b_rel_bias[j].astype(f32))
            o_ctx = (ctx_attention_a(pb, sink), ctx_attention_b(pb))
            o_lat = (window_attention_lat(pr, pb, sink, ka_c, va_c, j),
                     neighbourhood_attention_lat(pb, bias_tab, kb_c, vb_c, j))
        else:
            p = norm_mod_project(x, g_mix_pre[l], mod_l, _rec_weight(w_in_rec[j]))
            gates = gate_activations(p, c_a_log[j], c_dt_bias[j], d_b_i[j], d_b_f[j])
            g_ctx = _gates_rows(gates, 0, BATCH, SEQ)
            g_lat = _gates_rows(gates, N_CTX, DEC_BATCH, DEC_SEQ)
            conv_w = jnp.transpose(c_conv[j].astype(f32).reshape(3, 3 * HEADS, HD), (1, 0, 2))
            zc = jnp.zeros((BATCH, 2, HEADS, HD, HD), f32)
            oc_ctx, sc_ctx = delta_mixer(p, gates, g_ctx, conv_w, c_norm[j].astype(f32), zc, SEQ, 0, HPS_CTX)
            oc_lat, _ = delta_mixer(p, gates, g_lat, conv_w, c_norm[j].astype(f32), state_c[:, j].astype(f32),
                                    DEC_SEQ, N_CTX, HPS_LAT)
            od_ctx, cd_ctx, nd_ctx, md_ctx = mlstm_mixer(
                p, gates, g_ctx, d_norm[j].astype(f32), zc, jnp.zeros((BATCH, 2, HEADS, HD), f32),
                jnp.zeros((BATCH, 2, HEADS), f32), SEQ, 0, HPS_CTX)
            od_lat, _, _, _ = mlstm_mixer(
                p, gates, g_lat, d_norm[j].astype(f32), state_d_c[:, j].astype(f32), state_d_n[:, j].astype(f32),
                state_d_m[:, j].astype(f32), DEC_SEQ, N_CTX, HPS_LAT)
            rec_states.append((sc_ctx, cd_ctx, nd_ctx.reshape(BATCH, 2, HEADS, HD), md_ctx[:, :, :, 0, 0]))
            o_ctx = (oc_ctx, od_ctx)
            o_lat = (oc_lat, od_lat)
        x = out_project_residual(o_ctx, o_lat, w_out[l].astype(bf16), x, g_mix_post[l], mod_l)
        x = mlp_residual(x, g_ffn_pre[l], g_ffn_post[l], mod_l, w_up[l].astype(bf16), w_down[l].astype(bf16))

    y_prompt = x[:N_CTX].reshape(BATCH, SEQ, D_MODEL)
    y_sample = x[N_CTX:].reshape(DEC_BATCH, DEC_SEQ, D_MODEL)

    def kv(slab0, n):
        per = [jnp.transpose(a[slab0:slab0 + n].reshape(n, BATCH, SEQ, HD), (1, 2, 0, 3)) for a in att_kv]
        return jnp.stack(per, axis=1)

    new_a_k, new_a_v = kv(16, A_KV_HEADS), kv(18, A_KV_HEADS)
    new_b_k, new_b_v = kv(0, HEADS), kv(8, HEADS)
    new_state_c = jnp.stack([s[0] for s in rec_states], axis=1)
    new_state_d_c = jnp.stack([s[1] for s in rec_states], axis=1)
    new_state_d_n = jnp.stack([s[2] for s in rec_states], axis=1)
    new_state_d_m = jnp.stack([s[3] for s in rec_states], axis=1)
    return (y_prompt, y_sample, new_a_k, new_a_v, new_b_k, new_b_v, new_state_c, new_state_d_c, new_state_d_n,
            new_state_d_m)
```
